```python
import jax, jax.numpy as jnp
from jax import lax
import numpy as np

D_MODEL = 1024
BATCH = 1
SEQ = 16384
DEPTH = 1
DEC_BATCH = 16
DEC_SEQ = 16
PAST_LEN = 1024

CHUNK = 64
HEAD_DIM = 64
H_FOX = 8
H_RWKV = 8
FOX_WIDTH = H_FOX * HEAD_DIM
RWKV_WIDTH = H_RWKV * HEAD_DIM
MIX_WIDTH = FOX_WIDTH + RWKV_WIDTH
DECAY_LORA = 64
AAA_LORA = 64
GATE_LORA = 128
FOX_PROJ = 3 * FOX_WIDTH + H_FOX
RWKV_PROJ = 3 * RWKV_WIDTH + DECAY_LORA + AAA_LORA + GATE_LORA
PROJ_WIDTH = FOX_PROJ + RWKV_PROJ
D_FF = 2816
Q_BLOCK = 128
RMS_EPS = 1e-6
GN_EPS = 64e-5

kernel_name = "fox_rwkv7_macaron_stream_step"


def rms_norm(x, g):
    xf = x.astype(jnp.float32)
    y = xf * lax.rsqrt(jnp.mean(xf * xf, axis=-1, keepdims=True) + RMS_EPS)
    return (y * g.astype(jnp.float32)).astype(x.dtype)


def swiglu_ffn(x, w_in, w_out):
    gate, up = jnp.split(x @ w_in, 2, axis=-1)
    return (jax.nn.silu(gate) * up) @ w_out


def fox_attend(q, c_q, q_pos, k, v, c_k, k_pos):
    s = jnp.einsum('bqhd,bkhd->bhqk', q, k).astype(jnp.float32) * (HEAD_DIM ** -0.5)
    bias = jnp.transpose(c_q, (0, 2, 1))[..., :, None] - jnp.transpose(c_k, (0, 2, 1))[..., None, :]
    mask = k_pos[None, :] <= q_pos[:, None]
    s = jnp.where(mask, s + bias, -jnp.inf)
    p = jax.nn.softmax(s, axis=-1)
    return jnp.einsum('bhqk,bkhd->bqhd', p.astype(v.dtype), v)


def fox_prompt(q, k, v, c):
    B, T = q.shape[:2]
    nb = T // Q_BLOCK
    pos = jnp.arange(T)
    qb = q.reshape(B, nb, Q_BLOCK, H_FOX, HEAD_DIM).transpose(1, 0, 2, 3, 4)
    cb = c.reshape(B, nb, Q_BLOCK, H_FOX).transpose(1, 0, 2, 3)
    pb = pos.reshape(nb, Q_BLOCK)
    out = lax.map(lambda a: fox_attend(a[0], a[1], a[2], k, v, c, pos), (qb, cb, pb))
    return out.transpose(1, 0, 2, 3, 4).reshape(B, T, H_FOX, HEAD_DIM)


def rwkv7_step(s, inp):
    r, w, k, v, kk, b = inp
    sa = jnp.einsum('bhvk,bhk->bhv', s, -kk)
    s = s * w[:, :, None, :] + sa[..., :, None] * b[..., None, :] + v[..., :, None] * k[..., None, :]
    return s, jnp.einsum('bhvk,bhk->bhv', s, r)


def rwkv7_scan(s0, r, w, k, v, kk, b):
    B, T = r.shape[:2]
    blk = CHUNK if T % CHUNK == 0 else T

    def to_blocks(t):
        return jnp.moveaxis(t, 1, 0).reshape(T // blk, blk, B, H_RWKV, HEAD_DIM)

    xs = tuple(to_blocks(t) for t in (r, w, k, v, kk, b))

    def block_step(s, xb):
        return lax.scan(rwkv7_step, s, xb)

    s_fin, ys = lax.scan(block_step, s0, xs)
    return s_fin, jnp.moveaxis(ys.reshape(T, B, H_RWKV, HEAD_DIM), 0, 1)


def trunk_layer(x, past_k, past_v, past_logf, s0, shift0,
                norm_ffn1_g, ffn1_w_in, ffn1_w_out, norm_mix_g, w_in, w_out,
                fox_b_f, fox_q_norm_g, fox_k_norm_g,
                rwkv_mu, rwkv_w0, rwkv_w2, rwkv_a0, rwkv_a2, rwkv_g2,
                rwkv_k_k, rwkv_k_a, rwkv_r_k, rwkv_ln_g, rwkv_ln_b,
                norm_ffn2_g, ffn2_w_in, ffn2_w_out):
    B, T, _ = x.shape
    f32 = jnp.float32
    x = x + 0.5 * swiglu_ffn(rms_norm(x, norm_ffn1_g), ffn1_w_in, ffn1_w_out)
    h = rms_norm(x, norm_mix_g)
    proj = h @ w_in
    p_fox, p_rw = proj[..., :FOX_PROJ], proj[..., FOX_PROJ:]

    q = rms_norm(p_fox[..., :FOX_WIDTH].reshape(B, T, H_FOX, HEAD_DIM), fox_q_norm_g)
    k = rms_norm(p_fox[..., FOX_WIDTH:2 * FOX_WIDTH].reshape(B, T, H_FOX, HEAD_DIM), fox_k_norm_g)
    v = p_fox[..., 2 * FOX_WIDTH:3 * FOX_WIDTH].reshape(B, T, H_FOX, HEAD_DIM)
    logf = jax.nn.log_sigmoid((p_fox[..., 3 * FOX_WIDTH:] + fox_b_f).astype(f32))
    if past_k is None:
        o_fox = fox_prompt(q, k, v, jnp.cumsum(logf, axis=1))
    else:
        P = past_k.shape[1]
        k_all = jnp.concatenate([past_k.astype(k.dtype), k], axis=1)
        v_all = jnp.concatenate([past_v.astype(v.dtype), v], axis=1)
        c_all = jnp.cumsum(jnp.concatenate([past_logf.astype(f32), logf], axis=1), axis=1)
        pos = jnp.arange(P + T)
        o_fox = fox_attend(q, c_all[:, P:], pos[P:], k_all, v_all, c_all, pos)

    prev = jnp.concatenate([shift0.astype(p_rw.dtype), p_rw[:, :-1]], axis=1)
    xs = p_rw + (prev - p_rw) * rwkv_mu
    W = RWKV_WIDTH
    r, kr, vr, wd, ad, gd = jnp.split(
        xs, [W, 2 * W, 3 * W, 3 * W + DECAY_LORA, 3 * W + DECAY_LORA + AAA_LORA], axis=-1)
    w_log = -jax.nn.softplus(-(rwkv_w0 + jnp.tanh(wd) @ rwkv_w2)) - 0.5
    decay = jnp.exp(-jnp.exp(w_log.astype(f32)))
    a = jax.nn.sigmoid(rwkv_a0 + ad @ rwkv_a2)
    g = jax.nn.sigmoid(gd) @ rwkv_g2

    def heads(t):
        return t.reshape(B, T, H_RWKV, HEAD_DIM).astype(f32)

    kk = heads(kr * rwkv_k_k)
    kk = kk / jnp.maximum(jnp.sqrt(jnp.sum(kk * kk, axis=-1, keepdims=True)), 1e-12)
    kr = kr * (1 + (a - 1) * rwkv_k_a)
    r_h, k_h, v_h, a_h, w_h = heads(r), heads(kr), heads(vr), heads(a), heads(decay)
    s_new, y = rwkv7_scan(s0.astype(f32), r_h, w_h, k_h, v_h, kk, kk * a_h)
    mean = jnp.mean(y, axis=-1, keepdims=True)
    var = jnp.mean(jnp.square(y - mean), axis=-1, keepdims=True)
    y = ((y - mean) * lax.rsqrt(var + GN_EPS)).reshape(B, T, W) * rwkv_ln_g + rwkv_ln_b
    y = y + (jnp.sum(r_h * k_h * rwkv_r_k, axis=-1, keepdims=True) * v_h).reshape(B, T, W)
    o_rwkv = (y * g).astype(x.dtype)

    mix = jnp.concatenate([o_fox.reshape(B, T, FOX_WIDTH), o_rwkv], axis=-1) @ w_out
    x = x + mix
    x = x + 0.5 * swiglu_ffn(rms_norm(x, norm_ffn2_g), ffn2_w_in, ffn2_w_out)
    return x, (k, v, logf, s_new, p_rw[:, -1:])


def setup_inputs(seed: int = 0) -> dict:
    key = jax.random.key(seed)
    ks = iter(jax.random.split(key, 40))
    L = DEPTH

    def nrm(shape, scale):
        return jax.random.normal(next(ks), shape, jnp.float32) * scale

    def unif(shape, lo, hi):
        return jax.random.uniform(next(ks), shape, jnp.float32, lo, hi)

    return {
        "x_prompt": nrm((BATCH, SEQ, D_MODEL), 1.0),
        "x_sample": nrm((DEC_BATCH, DEC_SEQ, D_MODEL), 1.0),
        "cache_fox_k": nrm((L, DEC_BATCH, PAST_LEN, H_FOX, HEAD_DIM), 1.0),
        "cache_fox_v": nrm((L, DEC_BATCH, PAST_LEN, H_FOX, HEAD_DIM), 1.0),
        "cache_fox_logf": jax.nn.log_sigmoid(3.0 + nrm((L, DEC_BATCH, PAST_LEN, H_FOX), 0.5)),
        "state_rwkv": nrm((L, DEC_BATCH, H_RWKV, HEAD_DIM, HEAD_DIM), 0.1),
        "state_rwkv_shift": nrm((L, DEC_BATCH, 1, RWKV_PROJ), 1.0),
        "norm_ffn1_g": 1.0 + nrm((L, D_MODEL), 0.05),
        "ffn1_w_in": nrm((L, D_MODEL, 2 * D_FF), D_MODEL ** -0.5),
        "ffn1_w_out": nrm((L, D_FF, D_MODEL), D_FF ** -0.5),
        "norm_mix_g": 1.0 + nrm((L, D_MODEL), 0.05),
        "w_in": nrm((L, D_MODEL, PROJ_WIDTH), D_MODEL ** -0.5),
        "w_out": nrm((L, MIX_WIDTH, D_MODEL), MIX_WIDTH ** -0.5),
        "fox_b_f": 3.0 + nrm((L, H_FOX), 0.5),
        "fox_q_norm_g": 1.0 + nrm((L, HEAD_DIM), 0.05),
        "fox_k_norm_g": 1.0 + nrm((L, HEAD_DIM), 0.05),
        "rwkv_mu": unif((L, RWKV_PROJ), 0.2, 0.8),
        "rwkv_w0": unif((L, RWKV_WIDTH), -6.0, 1.0),
        "rwkv_w2": nrm((L, DECAY_LORA, RWKV_WIDTH), 0.5 * DECAY_LORA ** -0.5),
        "rwkv_a0": nrm((L, RWKV_WIDTH), 0.3),
        "rwkv_a2": nrm((L, AAA_LORA, RWKV_WIDTH), 0.5 * AAA_LORA ** -0.5),
        "rwkv_g2": nrm((L, GATE_LORA, RWKV_WIDTH), GATE_LORA ** -0.5),
        "rwkv_k_k": 0.85 + nrm((L, RWKV_WIDTH), 0.05),
        "rwkv_k_a": 1.0 + nrm((L, RWKV_WIDTH), 0.1),
        "rwkv_r_k": nrm((L, H_RWKV, HEAD_DIM), 0.1),
        "rwkv_ln_g": 1.0 + nrm((L, RWKV_WIDTH), 0.05),
        "rwkv_ln_b": nrm((L, RWKV_WIDTH), 0.02),
        "norm_ffn2_g": 1.0 + nrm((L, D_MODEL), 0.05),
        "ffn2_w_in": nrm((L, D_MODEL, 2 * D_FF), D_MODEL ** -0.5),
        "ffn2_w_out": nrm((L, D_FF, D_MODEL), D_FF ** -0.5),
    }


def reference(x_prompt, x_sample, cache_fox_k, cache_fox_v, cache_fox_logf, state_rwkv, state_rwkv_shift,
              norm_ffn1_g, ffn1_w_in, ffn1_w_out, norm_mix_g, w_in, w_out,
              fox_b_f, fox_q_norm_g, fox_k_norm_g,
              rwkv_mu, rwkv_w0, rwkv_w2, rwkv_a0, rwkv_a2, rwkv_g2,
              rwkv_k_k, rwkv_k_a, rwkv_r_k, rwkv_ln_g, rwkv_ln_b,
              norm_ffn2_g, ffn2_w_in, ffn2_w_out):
    weights = (norm_ffn1_g, ffn1_w_in, ffn1_w_out, norm_mix_g, w_in, w_out,
               fox_b_f, fox_q_norm_g, fox_k_norm_g,
               rwkv_mu, rwkv_w0, rwkv_w2, rwkv_a0, rwkv_a2, rwkv_g2,
               rwkv_k_k, rwkv_k_a, rwkv_r_k, rwkv_ln_g, rwkv_ln_b,
               norm_ffn2_g, ffn2_w_in, ffn2_w_out)
    bp = x_prompt.shape[0]
    yp, ys = x_prompt, x_sample
    prompt_states = ([], [], [], [], [])
    sample_states = ([], [], [], [], [])
    for l in range(DEPTH):
        lw = [w[l] for w in weights]
        s0 = jnp.zeros((bp, H_RWKV, HEAD_DIM, HEAD_DIM), jnp.float32)
        sh0 = jnp.zeros((bp, 1, RWKV_PROJ), x_prompt.dtype)
        yp, st_p = trunk_layer(yp, None, None, None, s0, sh0, *lw)
        ys, st_s = trunk_layer(ys, cache_fox_k[l], cache_fox_v[l], cache_fox_logf[l],
                               state_rwkv[l], state_rwkv_shift[l], *lw)
        for acc, st in zip(prompt_states, st_p):
            acc.append(st)
        for acc, st in zip(sample_states, st_s):
            acc.append(st)
    new_fox_k_prompt = jnp.stack(prompt_states[0])
    new_fox_v_prompt = jnp.stack(prompt_states[1])
    new_fox_logf_prompt = jnp.stack(prompt_states[2])
    new_rwkv_state_prompt = jnp.stack(prompt_states[3])
    new_rwkv_shift_prompt = jnp.stack(prompt_states[4])
    new_fox_k_sample = jnp.stack(sample_states[0])
    new_fox_v_sample = jnp.stack(sample_states[1])
    new_fox_logf_sample = jnp.stack(sample_states[2])
    new_rwkv_state_sample = jnp.stack(sample_states[3])
    new_rwkv_shift_sample = jnp.stack(sample_states[4])
    return (yp, ys,
            new_fox_k_prompt, new_fox_v_prompt, new_fox_logf_prompt, new_rwkv_state_prompt, new_rwkv_shift_prompt,
            new_fox_k_sample, new_fox_v_sample, new_fox_logf_sample, new_rwkv_state_sample, new_rwkv_shift_sample)
```

```python
import functools
import math

import jax
import jax.numpy as jnp
from jax import lax
from jax.experimental import pallas as pl
from jax.experimental.pallas import tpu as pltpu

F32 = jnp.float32
BF16 = jnp.bfloat16

D_MODEL = 1024
D_FF = 2816
HEAD_DIM = 64
N_HEADS = 8
WIDTH = N_HEADS * HEAD_DIM
RWKV_PROJ = 1792
RMS_EPS = 1e-6
GN_EPS = 64e-5
LOG2E = math.log2(math.e)

VMEM_LIMIT_BYTES = 56 * 1024 * 1024
FF_CHUNK = 512


def _const_spec(shape):
    return pl.BlockSpec(shape, lambda *_: (0,) * len(shape), pipeline_mode=pl.Buffered(1))


def _rms(x, g):
    return x * lax.rsqrt(jnp.mean(x * x, axis=-1, keepdims=True) + RMS_EPS) * g


def _swiglu_half_step(x, g, wg_ref, wu_ref, wo_ref):
    h = _rms(x, g).astype(BF16)
    acc = x
    for c0 in range(0, D_FF, FF_CHUNK):
        c1 = min(c0 + FF_CHUNK, D_FF)
        gate = jnp.dot(h, wg_ref[:, c0:c1], preferred_element_type=F32)
        up = jnp.dot(h, wu_ref[:, c0:c1], preferred_element_type=F32)
        a = (gate * jax.nn.sigmoid(gate) * up).astype(BF16)
        acc = acc + 0.5 * jnp.dot(a, wo_ref[c0:c1, :], preferred_element_type=F32)
    return acc


def _ffn_kernel(x_ref, g_ref, wg_ref, wu_ref, wo_ref, o_ref):
    o_ref[...] = _swiglu_half_step(x_ref[...], g_ref[...], wg_ref, wu_ref, wo_ref)


def _ffn(x, g, wg, wu, wo, tm):
    n = x.shape[0]
    return pl.pallas_call(
        _ffn_kernel,
        grid=(n // tm,),
        in_specs=[
            pl.BlockSpec((tm, D_MODEL), lambda i: (i, 0)),
            _const_spec((1, D_MODEL)),
            _const_spec((D_MODEL, D_FF)),
            _const_spec((D_MODEL, D_FF)),
            _const_spec((D_FF, D_MODEL)),
        ],
        out_specs=pl.BlockSpec((tm, D_MODEL), lambda i: (i, 0)),
        out_shape=jax.ShapeDtypeStruct((n, D_MODEL), F32),
        compiler_params=pltpu.CompilerParams(
            dimension_semantics=("arbitrary",), vmem_limit_bytes=VMEM_LIMIT_BYTES),
        name="ffn",
    )(x, g, wg, wu, wo)


PROJ_COLS = 3 * WIDTH + RWKV_PROJ + 128
N_SPLIT = 3
PAIR_LANES = 256
AUG_WIDTH = (N_HEADS // 2) * PAIR_LANES


def _split_bf16(x, n):
    parts = []
    for _ in range(n):
        p = x.astype(BF16)
        parts.append(p)
        x = x - p.astype(F32)
    return parts


def _head_sums(x, bd):
    hi, lo = _split_bf16(x, 2)
    return jnp.dot(hi, bd, preferred_element_type=F32) + jnp.dot(lo, bd, preferred_element_type=F32)


def _log_sigmoid(x):
    return jnp.minimum(x, 0.0) - jnp.log1p(jnp.exp(-jnp.abs(x)))


def _proj_kernel(x_ref, g_ref, w_ref, bf_ref, qg_ref, kg_ref, bd_ref, eq_ref, ek_ref, oneq_ref, onek_ref,
                 qaug_ref, k_ref, kaug_ref, v_ref, vt_ref, logf_ref, prw_ref, base_ref, carry_ref):
    i = pl.program_id(0)
    tm = x_ref.shape[0]

    @pl.when(i == 0)
    def _():
        carry_ref[...] = jnp.zeros_like(carry_ref)

    h = _rms(x_ref[...], g_ref[...]).astype(BF16)
    proj = jnp.dot(h, w_ref[...], preferred_element_type=F32)
    q = proj[:, :WIDTH]
    k = proj[:, WIDTH:2 * WIDTH]
    v = proj[:, 2 * WIDTH:3 * WIDTH]
    prw_ref[...] = proj[:, 3 * WIDTH:3 * WIDTH + RWKV_PROJ]
    f = proj[:, 3 * WIDTH + RWKV_PROJ:]

    bd = bd_ref[...]
    qn = q * lax.rsqrt(_head_sums(q * q, bd) * (1.0 / HEAD_DIM) + RMS_EPS) * qg_ref[...]
    kn = k * lax.rsqrt(_head_sums(k * k, bd) * (1.0 / HEAD_DIM) + RMS_EPS) * kg_ref[...]
    k_ref[...] = kn
    v_ref[...] = v
    vt_ref[0] = v.T.astype(BF16)

    lane = lax.broadcasted_iota(jnp.int32, f.shape, 1)
    logf = jnp.where(lane < N_HEADS, _log_sigmoid(f + bf_ref[...]), 0.0)
    logf_ref[...] = logf

    row = lax.broadcasted_iota(jnp.int32, (tm, tm), 0)
    col = lax.broadcasted_iota(jnp.int32, (tm, tm), 1)
    tri = jnp.where(col <= row, 1.0, 0.0).astype(BF16)
    cl = jnp.zeros_like(logf)
    for part in _split_bf16(logf, N_SPLIT):
        cl = cl + jnp.dot(tri, part, preferred_element_type=F32)
    c2 = cl * LOG2E
    base_ref[0] = jnp.broadcast_to(carry_ref[...], base_ref.shape[1:])
    carry_ref[...] = carry_ref[...] + c2[tm - 1:tm, :]

    parts = jnp.concatenate(_split_bf16(c2, N_SPLIT), axis=1)
    aug_q = (jnp.dot(parts, eq_ref[...], preferred_element_type=F32) + oneq_ref[...]).astype(BF16)
    aug_k = (jnp.dot(parts, ek_ref[...], preferred_element_type=F32) + onek_ref[...]).astype(BF16)
    qs = (qn * (HEAD_DIM ** -0.5 * LOG2E)).astype(BF16)
    ks = kn.astype(BF16)
    for p in range(N_HEADS // 2):
        qaug_ref[:, p * PAIR_LANES:p * PAIR_LANES + 128] = qs[:, p * 128:(p + 1) * 128]
        qaug_ref[:, p * PAIR_LANES + 128:(p + 1) * PAIR_LANES] = aug_q[:, p * 128:(p + 1) * 128]
        kaug_ref[:, p * PAIR_LANES:p * PAIR_LANES + 128] = ks[:, p * 128:(p + 1) * 128]
        kaug_ref[:, p * PAIR_LANES + 128:(p + 1) * PAIR_LANES] = aug_k[:, p * 128:(p + 1) * 128]


def _carrier_constants():
    eq = [[0.0] * WIDTH for _ in range(N_SPLIT * 128)]
    ek = [[0.0] * WIDTH for _ in range(N_SPLIT * 128)]
    oneq = [0.0] * WIDTH
    onek = [0.0] * WIDTH
    for hd in range(N_HEADS):
        off = (hd // 2) * 128 + (hd % 2) * 8
        for s in range(N_SPLIT):
            eq[s * 128 + hd][off + s] = 1.0
            ek[s * 128 + hd][off + N_SPLIT + s] = -1.0
            oneq[off + N_SPLIT + s] = 1.0
            onek[off + s] = 1.0
    return (jnp.array(eq, BF16), jnp.array(ek, BF16), jnp.array([oneq], F32), jnp.array([onek], F32))


def _prep_proj_weights(w_in, fox_b_f, q_norm_g, k_norm_g):
    fox_cols = 3 * WIDTH
    w_f = jnp.pad(w_in[:, fox_cols:fox_cols + N_HEADS], ((0, 0), (0, 128 - N_HEADS)))
    w_all = jnp.concatenate([w_in[:, :fox_cols], w_in[:, fox_cols + N_HEADS:], w_f], axis=1).astype(BF16)
    bf_pad = jnp.pad(fox_b_f, (0, 128 - N_HEADS))[None]
    return w_all, bf_pad, jnp.tile(q_norm_g, N_HEADS)[None], jnp.tile(k_norm_g, N_HEADS)[None]


def _proj(x, g, w_all, bf_pad, qg, kg, tm):
    n = x.shape[0]
    nblk = n // tm
    seg = jnp.arange(WIDTH) // HEAD_DIM
    bd = (seg[:, None] == seg[None, :]).astype(BF16)
    eq, ek, oneq, onek = _carrier_constants()
    row = lambda i: (i, 0)
    out_shape = (
        jax.ShapeDtypeStruct((n, AUG_WIDTH), BF16),
        jax.ShapeDtypeStruct((n, WIDTH), F32),
        jax.ShapeDtypeStruct((n, AUG_WIDTH), BF16),
        jax.ShapeDtypeStruct((n, WIDTH), F32),
        jax.ShapeDtypeStruct((nblk, WIDTH, tm), BF16),
        jax.ShapeDtypeStruct((n, 128), F32),
        jax.ShapeDtypeStruct((n, RWKV_PROJ), F32),
        jax.ShapeDtypeStruct((nblk, 8, 128), F32),
    )
    out_specs = (
        pl.BlockSpec((tm, AUG_WIDTH), row),
        pl.BlockSpec((tm, WIDTH), row),
        pl.BlockSpec((tm, AUG_WIDTH), row),
        pl.BlockSpec((tm, WIDTH), row),
        pl.BlockSpec((1, WIDTH, tm), lambda i: (i, 0, 0)),
        pl.BlockSpec((tm, 128), row),
        pl.BlockSpec((tm, RWKV_PROJ), row),
        pl.BlockSpec((1, 8, 128), lambda i: (i, 0, 0)),
    )
    return pl.pallas_call(
        _proj_kernel,
        grid=(nblk,),
        in_specs=[
            pl.BlockSpec((tm, D_MODEL), row),
            _const_spec((1, D_MODEL)),
            _const_spec((D_MODEL, PROJ_COLS)),
            _const_spec((1, 128)),
            _const_spec((1, WIDTH)),
            _const_spec((1, WIDTH)),
            _const_spec((WIDTH, WIDTH)),
            _const_spec((N_SPLIT * 128, WIDTH)),
            _const_spec((N_SPLIT * 128, WIDTH)),
            _const_spec((1, WIDTH)),
            _const_spec((1, WIDTH)),
        ],
        out_specs=out_specs,
        out_shape=out_shape,
        scratch_shapes=[pltpu.VMEM((1, 128), F32)],
        compiler_params=pltpu.CompilerParams(
            dimension_semantics=("arbitrary",), vmem_limit_bytes=VMEM_LIMIT_BYTES),
        name="mix_proj",
    )(x, g, w_all, bf_pad, qg, kg, bd, eq, ek, oneq, onek)


NEG_BIG = -1e30


def _attn_kernel(base_ref, q_ref, k_ref, vt_ref, o_ref, qt_ref, m_ref, l_ref, acc_ref):
    hd = pl.program_id(0)
    qi = pl.program_id(1)
    tq = q_ref.shape[0]
    tk = tq

    odd = hd % 2
    lane = lax.broadcasted_iota(jnp.int32, (tq, PAIR_LANES), 1)
    f0 = odd * HEAD_DIM
    c0 = 128 + odd * 8
    keep = ((lane >= f0) & (lane < f0 + HEAD_DIM)) | ((lane >= c0) & (lane < c0 + 2 * N_SPLIT))
    qm = jnp.where(keep, q_ref[...].astype(F32), 0.0)
    qt_ref[...] = qm.T.astype(BF16)

    def scores(j):
        kc = k_ref[pl.ds(pl.multiple_of(j * tk, tk), tk), :]
        s = jnp.dot(kc, qt_ref[...], preferred_element_type=F32)
        return s + (base_ref[hd, qi] - base_ref[hd, j])

    s = scores(qi)
    r = lax.broadcasted_iota(jnp.int32, (tk, tq), 0)
    c = lax.broadcasted_iota(jnp.int32, (tk, tq), 1)
    s = jnp.where(r <= c, s, NEG_BIG)
    m = jnp.max(s, axis=0, keepdims=True)
    p = jnp.exp2(s - m)
    m_ref[...] = m
    l_ref[...] = jnp.sum(p, axis=0, keepdims=True)
    acc_ref[...] = jnp.dot(vt_ref[qi], p.astype(BF16), preferred_element_type=F32)

    def body(j, carry):
        s = scores(j)
        m_old = m_ref[...]
        m_new = jnp.maximum(m_old, jnp.max(s, axis=0, keepdims=True))
        alpha = jnp.exp2(m_old - m_new)
        p = jnp.exp2(s - m_new)
        m_ref[...] = m_new
        l_ref[...] = alpha * l_ref[...] + jnp.sum(p, axis=0, keepdims=True)
        acc_ref[...] = alpha * acc_ref[...] + jnp.dot(vt_ref[j], p.astype(BF16), preferred_element_type=F32)
        return carry

    lax.fori_loop(0, qi, body, 0)
    o_ref[0] = (acc_ref[...] / l_ref[...]).astype(o_ref.dtype)


def _attention(base, qaug, kaug, vt, tq):
    n = qaug.shape[0]
    nblk = n // tq
    grid_spec = pltpu.PrefetchScalarGridSpec(
        num_scalar_prefetch=1,
        grid=(N_HEADS, nblk),
        in_specs=[
            pl.BlockSpec((tq, PAIR_LANES), lambda h, i, b: (i, h // 2)),
            pl.BlockSpec((n, PAIR_LANES), lambda h, i, b: (0, h // 2)),
            pl.BlockSpec((nblk, HEAD_DIM, tq), lambda h, i, b: (0, h, 0)),
        ],
        out_specs=pl.BlockSpec((1, HEAD_DIM, tq), lambda h, i, b: (i, h, 0)),
        scratch_shapes=[
            pltpu.VMEM((PAIR_LANES, tq), BF16),
            pltpu.VMEM((1, tq), F32),
            pltpu.VMEM((1, tq), F32),
            pltpu.VMEM((HEAD_DIM, tq), F32),
        ],
    )
    return pl.pallas_call(
        _attn_kernel,
        grid_spec=grid_spec,
        out_shape=jax.ShapeDtypeStruct((nblk, WIDTH, tq), BF16),
        compiler_params=pltpu.CompilerParams(
            dimension_semantics=("arbitrary", "arbitrary"), vmem_limit_bytes=VMEM_LIMIT_BYTES),
        name="fox_attention",
    )(base, qaug, kaug, vt)


HI = lax.Precision.HIGHEST
N_PAIRS = N_HEADS // 2


def _mm(a, b):
    return jnp.dot(a, b, preferred_element_type=F32, precision=HI)


def _mm_nt(a, b):
    return lax.dot_general(a, b, (((1,), (1,)), ((), ())), preferred_element_type=F32, precision=HI)


def _mm_tn(a, b):
    return lax.dot_general(a, b, (((0,), (0,)), ((), ())), preferred_element_type=F32, precision=HI)


def _rwkv_kernel(prw_ref, shift0_ref, s0_ref, mu_ref, w0_ref, a0_ref, w2a2_ref, g2_ref, kk_ref, ka_ref, rk_ref,
                 lng_ref, lnb_ref, bd_ref,
                 o_ref, sout_ref,
                 prev_ref, h_ref, at_ref, rt_ref, bt_ref, kt_ref, bh_ref, kh_ref, v_ref, gl_ref, y_ref,
                 *, chunk):
    t = pl.program_id(1)
    nt = pl.num_programs(1)
    tr = prw_ref.shape[0]
    n_chunks = tr // chunk

    @pl.when(t == 0)
    def _():
        prev_ref[...] = shift0_ref[0]
        h_ref[...] = s0_ref[0]

    prw = prw_ref[...]
    rolled = pltpu.roll(prw, 1, axis=0)
    row = lax.broadcasted_iota(jnp.int32, prw.shape, 0)
    prev = jnp.where(row == 0, prev_ref[...], rolled)
    prev_ref[...] = prw[tr - 1:tr, :]
    xs = prw + (prev - prw) * mu_ref[...]

    r = xs[:, :WIDTH]
    kr = xs[:, WIDTH:2 * WIDTH]
    vr = xs[:, 2 * WIDTH:3 * WIDTH]
    wa = xs[:, 3 * WIDTH:3 * WIDTH + 128]
    gd = xs[:, 3 * WIDTH + 128:]
    lane128 = lax.broadcasted_iota(jnp.int32, wa.shape, 1)
    wa_in = jnp.where(lane128 < 64, jnp.tanh(wa), wa).astype(BF16)
    lora = jnp.dot(wa_in, w2a2_ref[...], preferred_element_type=F32)
    w_log = _log_sigmoid(w0_ref[...] + lora[:, :WIDTH]) - 0.5
    lw = -jnp.exp(w_log)
    lr = jax.nn.sigmoid(a0_ref[...] + lora[:, WIDTH:])
    gate = jnp.dot(jax.nn.sigmoid(gd).astype(BF16), g2_ref[...], preferred_element_type=F32)

    bd = bd_ref[...]
    kk = kr * kk_ref[...]
    kk = kk / jnp.maximum(jnp.sqrt(_head_sums(kk * kk, bd)), 1e-12)
    k = kr * (1.0 + (lr - 1.0) * ka_ref[...])
    b = kk * lr

    ri = lax.broadcasted_iota(jnp.int32, (tr, tr), 0)
    ci = lax.broadcasted_iota(jnp.int32, (tr, tr), 1)
    tri = jnp.where((ci <= ri) & (ci // chunk == ri // chunk), 1.0, 0.0).astype(BF16)
    cs = jnp.zeros_like(lw)
    for part in _split_bf16(lw, N_SPLIT):
        cs = cs + jnp.dot(tri, part, preferred_element_type=F32)
    e_pos = jnp.exp(cs)
    e_neg = jnp.exp(-cs)
    at_ref[...] = -kk * jnp.exp(cs - lw)
    rt_ref[...] = r * e_pos
    bt_ref[...] = b * e_neg
    kt_ref[...] = k * e_neg
    v_ref[...] = vr
    for c in range(n_chunks):
        last = cs[(c + 1) * chunk - 1:(c + 1) * chunk, :]
        to_end = jnp.exp(last - cs[c * chunk:(c + 1) * chunk, :])
        bh_ref[c * chunk:(c + 1) * chunk, :] = b[c * chunk:(c + 1) * chunk, :] * to_end
        kh_ref[c * chunk:(c + 1) * chunk, :] = k[c * chunk:(c + 1) * chunk, :] * to_end
        gl_ref[c] = jnp.broadcast_to(jnp.exp(last), gl_ref.shape[1:])

    c2 = 2 * chunk
    lane = lax.broadcasted_iota(jnp.int32, (chunk, 128), 1)
    even = lane < HEAD_DIM

    def stack(x):
        return jnp.concatenate([jnp.where(even, x, 0.0), jnp.where(even, 0.0, x)], axis=0)

    rr = lax.broadcasted_iota(jnp.int32, (c2, c2), 0)
    cc = lax.broadcasted_iota(jnp.int32, (c2, c2), 1)
    same = (rr // chunk) == (cc // chunk)
    strict = same & (cc < rr)
    incl = same & (cc <= rr)
    eye = jnp.where(rr == cc, 1.0, 0.0)
    r128 = lax.broadcasted_iota(jnp.int32, (128, 128), 0)
    c128 = lax.broadcasted_iota(jnp.int32, (128, 128), 1)
    diag128 = r128 == c128

    def chunk_body(c, carry):
        rows = pl.ds(pl.multiple_of(c * chunk, chunk), chunk)
        for p in range(N_PAIRS):
            lanes = slice(p * 128, (p + 1) * 128)
            a_s = stack(at_ref[rows, lanes])
            r_s = stack(rt_ref[rows, lanes])
            b_s = stack(bt_ref[rows, lanes])
            k_s = stack(kt_ref[rows, lanes])
            v_s = stack(v_ref[rows, lanes])
            bh_s = stack(bh_ref[rows, lanes])
            kh_s = stack(kh_ref[rows, lanes])
            ar_s = jnp.concatenate([a_s, r_s], axis=0)
            gb = _mm_nt(ar_s, b_s)
            gk = _mm_nt(ar_s, k_s)
            a_ab = jnp.where(strict, gb[:c2], 0.0)
            a_ak = jnp.where(strict, gk[:c2], 0.0)
            a_rb = jnp.where(incl, gb[c2:], 0.0)
            a_rk = jnp.where(incl, gk[c2:], 0.0)
            inv = eye + a_ab
            pw = a_ab
            for _ in range(int(math.log2(chunk)) - 1):
                pw = _mm(pw, pw)
                inv = inv + _mm(inv, pw)
            h = h_ref[p]
            u = _mm(inv, _mm(a_s, h) + _mm(a_ak, v_s))
            y = _mm(r_s, h) + _mm(a_rb, u) + _mm(a_rk, v_s)
            y_ref[rows, lanes] = y[:chunk] + y[chunk:]
            g_col = jnp.sum(jnp.where(diag128, gl_ref[c][:, lanes][:1], 0.0), axis=1, keepdims=True)
            h_ref[p] = g_col * h + _mm_tn(bh_s, u) + _mm_tn(kh_s, v_s)
        return carry

    lax.fori_loop(0, n_chunks, chunk_body, 0)

    y = y_ref[...]
    mean = _head_sums(y, bd) * (1.0 / HEAD_DIM)
    yc = y - mean
    var = _head_sums(yc * yc, bd) * (1.0 / HEAD_DIM)
    yn = yc * lax.rsqrt(var + GN_EPS) * lng_ref[...] + lnb_ref[...]
    bonus = _head_sums(r * k * rk_ref[...], bd) * vr
    o_ref[...] = ((yn + bonus) * gate).astype(o_ref.dtype)

    @pl.when(t == nt - 1)
    def _():
        sout_ref[0] = h_ref[...]


def _rwkv(prw, shift0, s0bd, wts, nseq, tr, chunk):
    n = prw.shape[0]
    tiles = n // (nseq * tr)
    row = lambda s, t: (s * tiles + t, 0)
    vec = lambda w: _const_spec((1, w))
    scratch = [pltpu.VMEM((1, RWKV_PROJ), F32), pltpu.VMEM((N_PAIRS, 128, 128), F32)]
    scratch += [pltpu.VMEM((tr, WIDTH), F32) for _ in range(7)]
    scratch += [pltpu.VMEM((tr // chunk, 8, WIDTH), F32), pltpu.VMEM((tr, WIDTH), F32)]
    return pl.pallas_call(
        functools.partial(_rwkv_kernel, chunk=chunk),
        grid=(nseq, tiles),
        in_specs=[
            pl.BlockSpec((tr, RWKV_PROJ), row),
            pl.BlockSpec((1, 1, RWKV_PROJ), lambda s, t: (s, 0, 0)),
            pl.BlockSpec((1, N_PAIRS, 128, 128), lambda s, t: (s, 0, 0, 0)),
            vec(RWKV_PROJ), vec(WIDTH), vec(WIDTH),
            _const_spec((128, 2 * WIDTH)), _const_spec((128, WIDTH)),
            vec(WIDTH), vec(WIDTH), vec(WIDTH), vec(WIDTH), vec(WIDTH),
            _const_spec((WIDTH, WIDTH)),
        ],
        out_specs=(
            pl.BlockSpec((tr, WIDTH), row),
            pl.BlockSpec((1, N_PAIRS, 128, 128), lambda s, t: (s, 0, 0, 0)),
        ),
        out_shape=(
            jax.ShapeDtypeStruct((n, WIDTH), BF16),
            jax.ShapeDtypeStruct((nseq, N_PAIRS, 128, 128), F32),
        ),
        scratch_shapes=scratch,
        compiler_params=pltpu.CompilerParams(
            dimension_semantics=("arbitrary", "arbitrary"), vmem_limit_bytes=VMEM_LIMIT_BYTES),
        name="rwkv7_mix",
    )(prw, shift0, s0bd, *wts)


def _prep_rwkv_weights(mu, w0, w2, a0, a2, g2, k_k, k_a, r_k, ln_g, ln_b):
    z = jnp.zeros((64, WIDTH), F32)
    w2a2 = jnp.concatenate([jnp.concatenate([w2, z], axis=1), jnp.concatenate([z, a2], axis=1)], axis=0)
    seg = jnp.arange(WIDTH) // HEAD_DIM
    bd = (seg[:, None] == seg[None, :]).astype(BF16)
    return (mu[None], w0[None], a0[None], w2a2.astype(BF16), g2.astype(BF16), k_k[None], k_a[None],
            r_k.reshape(1, WIDTH), ln_g[None], ln_b[None], bd)


def _state_to_pairs(s):
    n = s.shape[0]
    ht = jnp.swapaxes(s, -1, -2).reshape(n, N_PAIRS, 2, HEAD_DIM, HEAD_DIM)
    z = jnp.zeros_like(ht[:, :, 0])
    top = jnp.concatenate([ht[:, :, 0], z], axis=-1)
    bot = jnp.concatenate([z, ht[:, :, 1]], axis=-1)
    return jnp.concatenate([top, bot], axis=-2)


def _pairs_to_state(hbd):
    n = hbd.shape[0]
    even = hbd[:, :, :HEAD_DIM, :HEAD_DIM]
    odd = hbd[:, :, HEAD_DIM:, HEAD_DIM:]
    ht = jnp.stack([even, odd], axis=2).reshape(n, N_HEADS, HEAD_DIM, HEAD_DIM)
    return jnp.swapaxes(ht, -1, -2)


def _attn_step_kernel(q_ref, kn_ref, vn_ref, lf_ref, ck_ref, cv_ref, clf_ref, o_ref):
    t = q_ref.shape[0]
    past = ck_ref.shape[1]
    mi = lax.broadcasted_iota(jnp.int32, (past, past), 0)
    ji = lax.broadcasted_iota(jnp.int32, (past, past), 1)
    after = jnp.where(mi > ji, 1.0, 0.0).astype(BF16)
    suffix = jnp.zeros(clf_ref.shape[1:], F32)
    for part in _split_bf16(clf_ref[0], N_SPLIT):
        suffix = suffix + jnp.dot(part, after, preferred_element_type=F32)
    suffix = suffix * LOG2E
    ri = lax.broadcasted_iota(jnp.int32, (t, t), 0)
    ci = lax.broadcasted_iota(jnp.int32, (t, t), 1)
    causal = ci <= ri
    tri = jnp.where(causal, 1.0, 0.0).astype(BF16)
    cn = jnp.zeros(lf_ref.shape, F32)
    for part in _split_bf16(lf_ref[...], N_SPLIT):
        cn = cn + jnp.dot(tri, part, preferred_element_type=F32)
    cn = cn * LOG2E
    cn_t = cn.T

    lane = lax.broadcasted_iota(jnp.int32, (t, 128), 1)
    for p in range(N_PAIRS):
        q_pair = q_ref[:, p * PAIR_LANES:p * PAIR_LANES + 128]
        lanes = slice(p * 128, (p + 1) * 128)
        k_past = ck_ref[0, :, lanes].astype(BF16)
        v_past = cv_ref[0, :, lanes].astype(BF16)
        k_new = kn_ref[:, lanes].astype(BF16)
        v_new = vn_ref[:, lanes].astype(BF16)
        outs = []
        for e in range(2):
            hd = 2 * p + e
            mine = (lane >= e * HEAD_DIM) & (lane < (e + 1) * HEAD_DIM)
            qm = jnp.where(mine, q_pair, jnp.zeros_like(q_pair))
            cq = cn[:, hd:hd + 1]
            s_past = lax.dot_general(qm, k_past, (((1,), (1,)), ((), ())), preferred_element_type=F32)
            s_past = s_past + cq + suffix[hd:hd + 1, :]
            s_new = lax.dot_general(qm, k_new, (((1,), (1,)), ((), ())), preferred_element_type=F32)
            s_new = jnp.where(causal, s_new + cq - cn_t[hd:hd + 1, :], NEG_BIG)
            m = jnp.maximum(jnp.max(s_past, axis=1, keepdims=True), jnp.max(s_new, axis=1, keepdims=True))
            p_past = jnp.exp2(s_past - m)
            p_new = jnp.exp2(s_new - m)
            denom = jnp.sum(p_past, axis=1, keepdims=True) + jnp.sum(p_new, axis=1, keepdims=True)
            pv = (jnp.dot(p_past.astype(BF16), v_past, preferred_element_type=F32)
                  + jnp.dot(p_new.astype(BF16), v_new, preferred_element_type=F32))
            outs.append(pv / denom)
        o_ref[:, lanes] = jnp.where(lane < HEAD_DIM, outs[0], outs[1]).astype(o_ref.dtype)


def _attention_step(qaug, kn, vn, logf, cache_k, cache_v, cache_logf_t, t):
    n = qaug.shape[0]
    nseq = n // t
    past = cache_k.shape[1]
    row = lambda b: (b, 0)
    return pl.pallas_call(
        _attn_step_kernel,
        grid=(nseq,),
        in_specs=[
            pl.BlockSpec((t, AUG_WIDTH), row),
            pl.BlockSpec((t, WIDTH), row),
            pl.BlockSpec((t, WIDTH), row),
            pl.BlockSpec((t, 128), row),
            pl.BlockSpec((1, past, WIDTH), lambda b: (b, 0, 0)),
            pl.BlockSpec((1, past, WIDTH), lambda b: (b, 0, 0)),
            pl.BlockSpec((1, N_HEADS, past), lambda b: (b, 0, 0)),
        ],
        out_specs=pl.BlockSpec((t, WIDTH), row),
        out_shape=jax.ShapeDtypeStruct((n, WIDTH), BF16),
        compiler_params=pltpu.CompilerParams(
            dimension_semantics=("arbitrary",), vmem_limit_bytes=VMEM_LIMIT_BYTES),
        name="fox_attention_step",
    )(qaug, kn, vn, logf, cache_k, cache_v, cache_logf_t)


def _out_kernel(x_ref, of_ref, orw_ref, wof_ref, worw_ref, g_ref, wg_ref, wu_ref, wo_ref, o_ref):
    mix = lax.dot_general(of_ref[0], wof_ref[...], (((0,), (0,)), ((), ())), preferred_element_type=F32)
    mix = mix + jnp.dot(orw_ref[...], worw_ref[...], preferred_element_type=F32)
    o_ref[...] = _swiglu_half_step(x_ref[...] + mix, g_ref[...], wg_ref, wu_ref, wo_ref)


def _out_ffn(x, o_fox_t, o_rwkv, wo_fox, wo_rwkv, g, wg, wu, wo, tm):
    n = x.shape[0]
    row = lambda i: (i, 0)
    return pl.pallas_call(
        _out_kernel,
        grid=(n // tm,),
        in_specs=[
            pl.BlockSpec((tm, D_MODEL), row),
            pl.BlockSpec((1, WIDTH, tm), lambda i: (i, 0, 0)),
            pl.BlockSpec((tm, WIDTH), row),
            _const_spec((WIDTH, D_MODEL)),
            _const_spec((WIDTH, D_MODEL)),
            _const_spec((1, D_MODEL)),
            _const_spec((D_MODEL, D_FF)),
            _const_spec((D_MODEL, D_FF)),
            _const_spec((D_FF, D_MODEL)),
        ],
        out_specs=pl.BlockSpec((tm, D_MODEL), row),
        out_shape=jax.ShapeDtypeStruct((n, D_MODEL), F32),
        compiler_params=pltpu.CompilerParams(
            dimension_semantics=("arbitrary",), vmem_limit_bytes=VMEM_LIMIT_BYTES),
        name="out_proj_ffn",
    )(x, o_fox_t, o_rwkv, wo_fox, wo_rwkv, g, wg, wu, wo)


PROMPT_TILE = 512
RWKV_TILE = 256
RWKV_CHUNK = 64


def kernel(x_prompt, x_sample, cache_fox_k, cache_fox_v, cache_fox_logf, state_rwkv, state_rwkv_shift, norm_ffn1_g, ffn1_w_in, ffn1_w_out, norm_mix_g, w_in, w_out, fox_b_f, fox_q_norm_g, fox_k_norm_g, rwkv_mu, rwkv_w0, rwkv_w2, rwkv_a0, rwkv_a2, rwkv_g2, rwkv_k_k, rwkv_k_a, rwkv_r_k, rwkv_ln_g, rwkv_ln_b, norm_ffn2_g, ffn2_w_in, ffn2_w_out):
    depth = norm_ffn1_g.shape[0]
    bp, seq, _ = x_prompt.shape
    bs, dec_seq, _ = x_sample.shape
    assert bp == 1, "the prompt path assumes a single stream"
    n_p, n_s = bp * seq, bs * dec_seq
    yp = x_prompt.reshape(n_p, D_MODEL)
    ys = x_sample.reshape(n_s, D_MODEL)
    prompt_states = ([], [], [], [], [])
    sample_states = ([], [], [], [], [])
    for l in range(depth):
        ffn1 = (norm_ffn1_g[l][None], ffn1_w_in[l, :, :D_FF].astype(BF16), ffn1_w_in[l, :, D_FF:].astype(BF16),
                ffn1_w_out[l].astype(BF16))
        ffn2 = (norm_ffn2_g[l][None], ffn2_w_in[l, :, :D_FF].astype(BF16), ffn2_w_in[l, :, D_FF:].astype(BF16),
                ffn2_w_out[l].astype(BF16))
        proj_w = _prep_proj_weights(w_in[l], fox_b_f[l], fox_q_norm_g[l], fox_k_norm_g[l])
        rwkv_w = _prep_rwkv_weights(rwkv_mu[l], rwkv_w0[l], rwkv_w2[l], rwkv_a0[l], rwkv_a2[l], rwkv_g2[l],
                                    rwkv_k_k[l], rwkv_k_a[l], rwkv_r_k[l], rwkv_ln_g[l], rwkv_ln_b[l])
        wo_fox = w_out[l, :WIDTH].astype(BF16)
        wo_rwkv = w_out[l, WIDTH:].astype(BF16)
        mix_g = norm_mix_g[l][None]

        x1 = _ffn(yp, *ffn1, PROMPT_TILE)
        qaug, kn, kaug, v, vt, logf, prw, base = _proj(x1, mix_g, *proj_w, PROMPT_TILE)
        o_fox_t = _attention(base[:, 0, :N_HEADS].T, qaug, kaug, vt, PROMPT_TILE)
        o_rwkv, s_pairs = _rwkv(prw, jnp.zeros((bp, 1, RWKV_PROJ), F32),
                                jnp.zeros((bp, N_PAIRS, 128, 128), F32), rwkv_w, bp, RWKV_TILE, RWKV_CHUNK)
        yp = _out_ffn(x1, o_fox_t, o_rwkv, wo_fox, wo_rwkv, *ffn2, PROMPT_TILE)
        for acc, st in zip(prompt_states, (
                kn.reshape(bp, seq, N_HEADS, HEAD_DIM), v.reshape(bp, seq, N_HEADS, HEAD_DIM),
                logf[:, :N_HEADS].reshape(bp, seq, N_HEADS), _pairs_to_state(s_pairs),
                prw.reshape(bp, seq, RWKV_PROJ)[:, -1:])):
            acc.append(st)

        x1 = _ffn(ys, *ffn1, n_s)
        qaug, kn, kaug, v, vt, logf, prw, base = _proj(x1, mix_g, *proj_w, n_s)
        past = cache_fox_k.shape[2]
        o_fox = _attention_step(
            qaug, kn, v, logf, cache_fox_k[l].reshape(bs, past, WIDTH), cache_fox_v[l].reshape(bs, past, WIDTH),
            jnp.swapaxes(cache_fox_logf[l], 1, 2), dec_seq)
        o_rwkv, s_pairs = _rwkv(prw, state_rwkv_shift[l], _state_to_pairs(state_rwkv[l]), rwkv_w,
                                bs, dec_seq, dec_seq)
        ys = _out_ffn(x1, o_fox.T[None], o_rwkv, wo_fox, wo_rwkv, *ffn2, n_s)
        for acc, st in zip(sample_states, (
                kn.reshape(bs, dec_seq, N_HEADS, HEAD_DIM), v.reshape(bs, dec_seq, N_HEADS, HEAD_DIM),
                logf[:, :N_HEADS].reshape(bs, dec_seq, N_HEADS), _pairs_to_state(s_pairs),
                prw.reshape(bs, dec_seq, RWKV_PROJ)[:, -1:])):
            acc.append(st)

    return (yp.reshape(bp, seq, D_MODEL), ys.reshape(bs, dec_seq, D_MODEL),
            *(jnp.stack(a) for a in prompt_states), *(jnp.stack(a) for a in sample_states))
```

```python
import functools
import math

import jax
import jax.numpy as jnp
from jax import lax
from jax.experimental import pallas as pl
from jax.experimental.pallas import tpu as pltpu

F32 = jnp.float32
BF16 = jnp.bfloat16

D_MODEL = 1024
D_FF = 2816
HEAD_DIM = 64
N_HEADS = 8
WIDTH = N_HEADS * HEAD_DIM
RWKV_PROJ = 1792
RMS_EPS = 1e-6
GN_EPS = 64e-5
LOG2E = math.log2(math.e)

VMEM_LIMIT_BYTES = 56 * 1024 * 1024
FF_CHUNK = 512


def _const_spec(shape):
    return pl.BlockSpec(shape, lambda *_: (0,) * len(shape), pipeline_mode=pl.Buffered(1))


def _rms(x, g):
    return x * lax.rsqrt(jnp.mean(x * x, axis=-1, keepdims=True) + RMS_EPS) * g


def _swiglu_half_step(x, g, wg_ref, wu_ref, wo_ref):
    h = _rms(x, g).astype(BF16)
    acc = x
    for c0 in range(0, D_FF, FF_CHUNK):
        c1 = min(c0 + FF_CHUNK, D_FF)
        gate = jnp.dot(h, wg_ref[:, c0:c1], preferred_element_type=F32)
        up = jnp.dot(h, wu_ref[:, c0:c1], preferred_element_type=F32)
        a = (gate * jax.nn.sigmoid(gate) * up).astype(BF16)
        acc = acc + 0.5 * jnp.dot(a, wo_ref[c0:c1, :], preferred_element_type=F32)
    return acc


def _ffn_kernel(x_ref, g_ref, wg_ref, wu_ref, wo_ref, o_ref):
    o_ref[...] = _swiglu_half_step(x_ref[...], g_ref[...], wg_ref, wu_ref, wo_ref)


def _ffn(x, g, wg, wu, wo, tm):
    n = x.shape[0]
    return pl.pallas_call(
        _ffn_kernel,
        grid=(n // tm,),
        in_specs=[
            pl.BlockSpec((tm, D_MODEL), lambda i: (i, 0)),
            _const_spec((1, D_MODEL)),
            _const_spec((D_MODEL, D_FF)),
            _const_spec((D_MODEL, D_FF)),
            _const_spec((D_FF, D_MODEL)),
        ],
        out_specs=pl.BlockSpec((tm, D_MODEL), lambda i: (i, 0)),
        out_shape=jax.ShapeDtypeStruct((n, D_MODEL), F32),
        compiler_params=pltpu.CompilerParams(
            dimension_semantics=("arbitrary",), vmem_limit_bytes=VMEM_LIMIT_BYTES),
        name="ffn",
    )(x, g, wg, wu, wo)


PROJ_COLS = 3 * WIDTH + RWKV_PROJ + 128
N_SPLIT = 3
PAIR_LANES = 256
AUG_WIDTH = (N_HEADS // 2) * PAIR_LANES


def _split_bf16(x, n):
    parts = []
    for _ in range(n):
        p = x.astype(BF16)
        parts.append(p)
        x = x - p.astype(F32)
    return parts


def _head_sums(x, bd):
    hi, lo = _split_bf16(x, 2)
    return jnp.dot(hi, bd, preferred_element_type=F32) + jnp.dot(lo, bd, preferred_element_type=F32)


def _log_sigmoid(x):
    return jnp.minimum(x, 0.0) - jnp.log1p(jnp.exp(-jnp.abs(x)))


def _proj_kernel(x_ref, g_ref, w_ref, bf_ref, qg_ref, kg_ref, bd_ref, eq_ref, ek_ref, oneq_ref, onek_ref,
                 qaug_ref, k_ref, kaug_ref, v_ref, vt_ref, logf_ref, prw_ref, base_ref, carry_ref):
    i = pl.program_id(0)
    tm = x_ref.shape[0]

    @pl.when(i == 0)
    def _():
        carry_ref[...] = jnp.zeros_like(carry_ref)

    h = _rms(x_ref[...], g_ref[...]).astype(BF16)
    proj = jnp.dot(h, w_ref[...], preferred_element_type=F32)
    q = proj[:, :WIDTH]
    k = proj[:, WIDTH:2 * WIDTH]
    v = proj[:, 2 * WIDTH:3 * WIDTH]
    prw_ref[...] = proj[:, 3 * WIDTH:3 * WIDTH + RWKV_PROJ]
    f = proj[:, 3 * WIDTH + RWKV_PROJ:]

    bd = bd_ref[...]
    qn = q * lax.rsqrt(_head_sums(q * q, bd) * (1.0 / HEAD_DIM) + RMS_EPS) * qg_ref[...]
    kn = k * lax.rsqrt(_head_sums(k * k, bd) * (1.0 / HEAD_DIM) + RMS_EPS) * kg_ref[...]
    k_ref[...] = kn
    v_ref[...] = v
    vt_ref[0] = v.T.astype(BF16)

    lane = lax.broadcasted_iota(jnp.int32, f.shape, 1)
    logf = jnp.where(lane < N_HEADS, _log_sigmoid(f + bf_ref[...]), 0.0)
    logf_ref[...] = logf

    row = lax.broadcasted_iota(jnp.int32, (tm, tm), 0)
    col = lax.broadcasted_iota(jnp.int32, (tm, tm), 1)
    tri = jnp.where(col <= row, 1.0, 0.0).astype(BF16)
    cl = jnp.zeros_like(logf)
    for part in _split_bf16(logf, N_SPLIT):
        cl = cl + jnp.dot(tri, part, preferred_element_type=F32)
    c2 = cl * LOG2E
    base_ref[0] = jnp.broadcast_to(carry_ref[...], base_ref.shape[1:])
    carry_ref[...] = carry_ref[...] + c2[tm - 1:tm, :]

    parts = jnp.concatenate(_split_bf16(c2, N_SPLIT), axis=1)
    aug_q = (jnp.dot(parts, eq_ref[...], preferred_element_type=F32) + oneq_ref[...]).astype(BF16)
    aug_k = (jnp.dot(parts, ek_ref[...], preferred_element_type=F32) + onek_ref[...]).astype(BF16)
    qs = (qn * (HEAD_DIM ** -0.5 * LOG2E)).astype(BF16)
    ks = kn.astype(BF16)
    for p in range(N_HEADS // 2):
        qaug_ref[:, p * PAIR_LANES:p * PAIR_LANES + 128] = qs[:, p * 128:(p + 1) * 128]
        qaug_ref[:, p * PAIR_LANES + 128:(p + 1) * PAIR_LANES] = aug_q[:, p * 128:(p + 1) * 128]
        kaug_ref[:, p * PAIR_LANES:p * PAIR_LANES + 128] = ks[:, p * 128:(p + 1) * 128]
        kaug_ref[:, p * PAIR_LANES + 128:(p + 1) * PAIR_LANES] = aug_k[:, p * 128:(p + 1) * 128]


def _carrier_constants():
    eq = [[0.0] * WIDTH for _ in range(N_SPLIT * 128)]
    ek = [[0.0] * WIDTH for _ in range(N_SPLIT * 128)]
    oneq = [0.0] * WIDTH
    onek = [0.0] * WIDTH
    for hd in range(N_HEADS):
        off = (hd // 2) * 128 + (hd % 2) * 8
        for s in range(N_SPLIT):
            eq[s * 128 + hd][off + s] = 1.0
            ek[s * 128 + hd][off + N_SPLIT + s] = -1.0
            oneq[off + N_SPLIT + s] = 1.0
            onek[off + s] = 1.0
    return (jnp.array(eq, BF16), jnp.array(ek, BF16), jnp.array([oneq], F32), jnp.array([onek], F32))


def _prep_proj_weights(w_in, fox_b_f, q_norm_g, k_norm_g):
    fox_cols = 3 * WIDTH
    w_f = jnp.pad(w_in[:, fox_cols:fox_cols + N_HEADS], ((0, 0), (0, 128 - N_HEADS)))
    w_all = jnp.concatenate([w_in[:, :fox_cols], w_in[:, fox_cols + N_HEADS:], w_f], axis=1).astype(BF16)
    bf_pad = jnp.pad(fox_b_f, (0, 128 - N_HEADS))[None]
    return w_all, bf_pad, jnp.tile(q_norm_g, N_HEADS)[None], jnp.tile(k_norm_g, N_HEADS)[None]


def _proj(x, g, w_all, bf_pad, qg, kg, tm):
    n = x.shape[0]
    nblk = n // tm
    seg = jnp.arange(WIDTH) // HEAD_DIM
    bd = (seg[:, None] == seg[None, :]).astype(BF16)
    eq, ek, oneq, onek = _carrier_constants()
    row = lambda i: (i, 0)
    out_shape = (
        jax.ShapeDtypeStruct((n, AUG_WIDTH), BF16),
        jax.ShapeDtypeStruct((n, WIDTH), F32),
        jax.ShapeDtypeStruct((n, AUG_WIDTH), BF16),
        jax.ShapeDtypeStruct((n, WIDTH), F32),
        jax.ShapeDtypeStruct((nblk, WIDTH, tm), BF16),
        jax.ShapeDtypeStruct((n, 128), F32),
        jax.ShapeDtypeStruct((n, RWKV_PROJ), F32),
        jax.ShapeDtypeStruct((nblk, 8, 128), F32),
    )
    out_specs = (
        pl.BlockSpec((tm, AUG_WIDTH), row),
        pl.BlockSpec((tm, WIDTH), row),
        pl.BlockSpec((tm, AUG_WIDTH), row),
        pl.BlockSpec((tm, WIDTH), row),
        pl.BlockSpec((1, WIDTH, tm), lambda i: (i, 0, 0)),
        pl.BlockSpec((tm, 128), row),
        pl.BlockSpec((tm, RWKV_PROJ), row),
        pl.BlockSpec((1, 8, 128), lambda i: (i, 0, 0)),
    )
    return pl.pallas_call(
        _proj_kernel,
        grid=(nblk,),
        in_specs=[
            pl.BlockSpec((tm, D_MODEL), row),
            _const_spec((1, D_MODEL)),
            _const_spec((D_MODEL, PROJ_COLS)),
            _const_spec((1, 128)),
            _const_spec((1, WIDTH)),
            _const_spec((1, WIDTH)),
            _const_spec((WIDTH, WIDTH)),
            _const_spec((N_SPLIT * 128, WIDTH)),
            _const_spec((N_SPLIT * 128, WIDTH)),
            _const_spec((1, WIDTH)),
            _const_spec((1, WIDTH)),
        ],
        out_specs=out_specs,
        out_shape=out_shape,
        scratch_shapes=[pltpu.VMEM((1, 128), F32)],
        compiler_params=pltpu.CompilerParams(
            dimension_semantics=("arbitrary",), vmem_limit_bytes=VMEM_LIMIT_BYTES),
        name="mix_proj",
    )(x, g, w_all, bf_pad, qg, kg, bd, eq, ek, oneq, onek)


NEG_BIG = -1e30


def _attn_kernel(base_ref, q_ref, k_ref, vt_ref, o_ref,
                 qt_ref, s0_ref, s1_ref, p0_ref, p1_ref, m_ref, l_ref, alpha_ref, acc_ref):
    pair = pl.program_id(0)
    qi = pl.program_id(1)
    tq = q_ref.shape[0]
    tk = tq
    n_steps = qi + 1

    lane = lax.broadcasted_iota(jnp.int32, (tq, PAIR_LANES), 1)
    q = q_ref[...].astype(F32)
    for e in range(2):
        f0 = e * HEAD_DIM
        c0 = 128 + e * 8
        keep = ((lane >= f0) & (lane < f0 + HEAD_DIM)) | ((lane >= c0) & (lane < c0 + 2 * N_SPLIT))
        qt_ref[:, e * tq:(e + 1) * tq] = jnp.where(keep, q, 0.0).T.astype(BF16)

    col = lax.broadcasted_iota(jnp.int32, (1, 2 * tq), 1)

    def block_of(step):
        return jnp.where(step == 0, qi, step - 1)

    def offset(step):
        blk = block_of(step)
        d_e = base_ref[2 * pair, qi] - base_ref[2 * pair, blk]
        d_o = base_ref[2 * pair + 1, qi] - base_ref[2 * pair + 1, blk]
        d = jnp.where(col < tq, d_e, d_o)
        return jnp.where(step < n_steps, d, NEG_BIG)

    def scores(step):
        blk = block_of(jnp.minimum(step, qi + 1))
        kc = k_ref[pl.ds(pl.multiple_of(blk * tk, tk), tk), :]
        return jnp.dot(kc, qt_ref[...], preferred_element_type=F32)

    def softmax(step, s_ref, p_ref):
        s = s_ref[...]
        d = offset(step)
        m_old = m_ref[...]
        m_new = jnp.maximum(m_old, jnp.max(s, axis=0, keepdims=True) + d)
        alpha = jnp.exp2(m_old - m_new)
        p = jnp.exp2(s - (m_new - d))
        m_ref[...] = m_new
        alpha_ref[...] = alpha
        l_ref[...] = alpha * l_ref[...] + jnp.sum(p, axis=0, keepdims=True)
        p_ref[...] = p.astype(BF16)

    def accumulate(step, p_ref):
        vt = vt_ref[block_of(jnp.maximum(step, 0))]
        alpha = alpha_ref[...]
        for e in range(2):
            cols = slice(e * tq, (e + 1) * tq)
            pv = jnp.dot(vt[e * HEAD_DIM:(e + 1) * HEAD_DIM, :], p_ref[:, cols], preferred_element_type=F32)
            acc_ref[e] = alpha[:, cols] * acc_ref[e] + pv

    r = lax.broadcasted_iota(jnp.int32, (tk, 2 * tq), 0)
    c = lax.broadcasted_iota(jnp.int32, (tk, 2 * tq), 1)
    qpos = jnp.where(c < tq, c, c - tq)
    s0_ref[...] = jnp.where(r <= qpos, scores(0), NEG_BIG)
    m_ref[...] = jnp.full(m_ref.shape, NEG_BIG, F32)
    l_ref[...] = jnp.zeros_like(l_ref)
    alpha_ref[...] = jnp.ones_like(alpha_ref)
    acc_ref[...] = jnp.zeros_like(acc_ref)
    p1_ref[...] = jnp.zeros_like(p1_ref)

    def two_steps(it, carry):
        k = 2 * it
        accumulate(k - 1, p1_ref)
        s1_ref[...] = scores(k + 1)
        softmax(k, s0_ref, p0_ref)
        accumulate(k, p0_ref)
        s0_ref[...] = scores(k + 2)
        softmax(k + 1, s1_ref, p1_ref)
        return carry

    n_trips = (n_steps + 1) // 2
    lax.fori_loop(0, n_trips, two_steps, 0)
    accumulate(2 * n_trips - 1, p1_ref)
    inv_l = 1.0 / l_ref[...]
    for e in range(2):
        o_ref[0, e * HEAD_DIM:(e + 1) * HEAD_DIM, :] = (acc_ref[e] * inv_l[:, e * tq:(e + 1) * tq]).astype(o_ref.dtype)


def _attention(base, qaug, kaug, vt, tq):
    n = qaug.shape[0]
    nblk = n // tq
    grid_spec = pltpu.PrefetchScalarGridSpec(
        num_scalar_prefetch=1,
        grid=(N_HEADS // 2, nblk),
        in_specs=[
            pl.BlockSpec((tq, PAIR_LANES), lambda p, i, b: (i, p)),
            pl.BlockSpec((n, PAIR_LANES), lambda p, i, b: (0, p)),
            pl.BlockSpec((nblk, 2 * HEAD_DIM, tq), lambda p, i, b: (0, p, 0)),
        ],
        out_specs=pl.BlockSpec((1, 2 * HEAD_DIM, tq), lambda p, i, b: (i, p, 0)),
        scratch_shapes=[
            pltpu.VMEM((PAIR_LANES, 2 * tq), BF16),
            pltpu.VMEM((tq, 2 * tq), F32),
            pltpu.VMEM((tq, 2 * tq), F32),
            pltpu.VMEM((tq, 2 * tq), BF16),
            pltpu.VMEM((tq, 2 * tq), BF16),
            pltpu.VMEM((1, 2 * tq), F32),
            pltpu.VMEM((1, 2 * tq), F32),
            pltpu.VMEM((1, 2 * tq), F32),
            pltpu.VMEM((2, HEAD_DIM, tq), F32),
        ],
    )
    return pl.pallas_call(
        _attn_kernel,
        grid_spec=grid_spec,
        out_shape=jax.ShapeDtypeStruct((nblk, WIDTH, tq), BF16),
        compiler_params=pltpu.CompilerParams(
            dimension_semantics=("arbitrary", "arbitrary"), vmem_limit_bytes=VMEM_LIMIT_BYTES),
        name="fox_attention",
    )(base, qaug, kaug, vt)


N_PAIRS = N_HEADS // 2


def _mm(a, b):
    return jnp.dot(a.astype(BF16), b.astype(BF16), preferred_element_type=F32)


def _mm_nt(a, b):
    return lax.dot_general(a.astype(BF16), b.astype(BF16), (((1,), (1,)), ((), ())), preferred_element_type=F32)


def _mm_tn(a, b):
    return lax.dot_general(a.astype(BF16), b.astype(BF16), (((0,), (0,)), ((), ())), preferred_element_type=F32)


def _rwkv_kernel(prw_ref, shift0_ref, s0_ref, mu_ref, w0_ref, a0_ref, w2a2_ref, g2_ref, kk_ref, ka_ref, rk_ref,
                 lng_ref, lnb_ref, bd_ref,
                 o_ref, sout_ref,
                 prev_ref, h_ref, y_ref,
                 *, chunk):
    t = pl.program_id(1)
    nt = pl.num_programs(1)
    tr = prw_ref.shape[0]
    n_chunks = tr // chunk

    @pl.when(t == 0)
    def _():
        prev_ref[...] = shift0_ref[0]
        h_ref[...] = s0_ref[0]

    prw = prw_ref[...]
    rolled = pltpu.roll(prw, 1, axis=0)
    row = lax.broadcasted_iota(jnp.int32, prw.shape, 0)
    prev = jnp.where(row == 0, prev_ref[...], rolled)
    prev_ref[...] = prw[tr - 1:tr, :]
    xs = prw + (prev - prw) * mu_ref[...]

    r = xs[:, :WIDTH]
    kr = xs[:, WIDTH:2 * WIDTH]
    vr = xs[:, 2 * WIDTH:3 * WIDTH]
    wa = xs[:, 3 * WIDTH:3 * WIDTH + 128]
    gd = xs[:, 3 * WIDTH + 128:]
    lane128 = lax.broadcasted_iota(jnp.int32, wa.shape, 1)
    wa_in = jnp.where(lane128 < 64, jnp.tanh(wa), wa).astype(BF16)
    lora = jnp.dot(wa_in, w2a2_ref[...], preferred_element_type=F32)
    w_log = _log_sigmoid(w0_ref[...] + lora[:, :WIDTH]) - 0.5
    lw = -jnp.exp(w_log)
    lr = jax.nn.sigmoid(a0_ref[...] + lora[:, WIDTH:])
    gate = jnp.dot(jax.nn.sigmoid(gd).astype(BF16), g2_ref[...], preferred_element_type=F32)

    bd = bd_ref[...]
    kk = kr * kk_ref[...]
    kk = kk / jnp.maximum(jnp.sqrt(_head_sums(kk * kk, bd)), 1e-12)
    k = kr * (1.0 + (lr - 1.0) * ka_ref[...])
    b = kk * lr

    ri = lax.broadcasted_iota(jnp.int32, (tr, tr), 0)
    ci = lax.broadcasted_iota(jnp.int32, (tr, tr), 1)
    tri = jnp.where((ci <= ri) & (ci // chunk == ri // chunk), 1.0, 0.0).astype(BF16)
    cs = jnp.zeros_like(lw)
    for part in _split_bf16(lw, N_SPLIT):
        cs = cs + jnp.dot(tri, part, preferred_element_type=F32)
    e_pos = jnp.exp(cs)
    e_neg = jnp.exp(-cs)
    at = (-kk * jnp.exp(cs - lw)).astype(BF16)
    rt = (r * e_pos).astype(BF16)
    bt = (b * e_neg).astype(BF16)
    kt = (k * e_neg).astype(BF16)
    vb = vr.astype(BF16)
    g_last, bh_rows, kh_rows = [], [], []
    for c in range(n_chunks):
        rows = slice(c * chunk, (c + 1) * chunk)
        last = cs[(c + 1) * chunk - 1:(c + 1) * chunk, :]
        to_end = jnp.exp(last - cs[rows, :])
        bh_rows.append((b[rows, :] * to_end).astype(BF16))
        kh_rows.append((k[rows, :] * to_end).astype(BF16))
        g_last.append(jnp.exp(last))

    c2 = 2 * chunk
    lane = lax.broadcasted_iota(jnp.int32, (chunk, 128), 1)
    even = lane < HEAD_DIM

    def stack(x, c, p):
        x = x[c * chunk:(c + 1) * chunk, p * 128:(p + 1) * 128]
        zero = jnp.zeros_like(x)
        return jnp.concatenate([jnp.where(even, x, zero), jnp.where(even, zero, x)], axis=0)

    rr = lax.broadcasted_iota(jnp.int32, (c2, c2), 0)
    cc = lax.broadcasted_iota(jnp.int32, (c2, c2), 1)
    same = (rr // chunk) == (cc // chunk)
    strict = same & (cc < rr)
    incl = same & (cc <= rr)
    eye = jnp.where(rr == cc, 1.0, 0.0)
    r128 = lax.broadcasted_iota(jnp.int32, (128, 128), 0)
    c128 = lax.broadcasted_iota(jnp.int32, (128, 128), 1)
    diag128 = r128 == c128

    insts = [(c, p) for c in range(n_chunks) for p in range(N_PAIRS)]
    a_s = [stack(at, c, p) for c, p in insts]
    r_s = [stack(rt, c, p) for c, p in insts]
    b_s = [stack(bt, c, p) for c, p in insts]
    k_s = [stack(kt, c, p) for c, p in insts]
    v_s = [stack(vb, c, p) for c, p in insts]
    bh_s = [stack(bh_rows[c], 0, p) for c, p in insts]
    kh_s = [stack(kh_rows[c], 0, p) for c, p in insts]
    ar_s = [jnp.concatenate([a, rr_], axis=0) for a, rr_ in zip(a_s, r_s)]
    gb = [_mm_nt(x, y) for x, y in zip(ar_s, b_s)]
    gk = [_mm_nt(x, y) for x, y in zip(ar_s, k_s)]
    a_ab = [jnp.where(strict, g[:c2], 0.0) for g in gb]
    a_ak = [jnp.where(strict, g[:c2], 0.0).astype(BF16) for g in gk]
    a_rb = [jnp.where(incl, g[c2:], 0.0).astype(BF16) for g in gb]
    a_rk = [jnp.where(incl, g[c2:], 0.0).astype(BF16) for g in gk]
    inv = [eye + x for x in a_ab]
    pw = [x.astype(BF16) for x in a_ab]
    for _ in range(int(math.log2(chunk)) - 1):
        pw = [_mm(x, x).astype(BF16) for x in pw]
        inv = [i + _mm(i, x) for i, x in zip(inv, pw)]
    inv = [i.astype(BF16) for i in inv]
    x1 = [_mm(m, v) for m, v in zip(a_ak, v_s)]
    wm = [_mm(i, a).astype(BF16) for i, a in zip(inv, a_s)]
    u0 = [_mm(i, x).astype(BF16) for i, x in zip(inv, x1)]
    mb = [_mm_tn(bh, w).astype(BF16) for bh, w in zip(bh_s, wm)]
    rm = [(rr_.astype(F32) + _mm(m, w)).astype(BF16) for rr_, m, w in zip(r_s, a_rb, wm)]
    y0 = [_mm(m1, u) + _mm(m2, v) for m1, u, m2, v in zip(a_rb, u0, a_rk, v_s)]
    n0 = [_mm_tn(bh, u) + _mm_tn(kh, v) for bh, u, kh, v in zip(bh_s, u0, kh_s, v_s)]

    states = [h_ref[p] for p in range(N_PAIRS)]
    for i, (c, p) in enumerate(insts):
        h = states[p]
        hb = h.astype(BF16)
        y = _mm(rm[i], hb) + y0[i]
        y_ref[c * chunk:(c + 1) * chunk, p * 128:(p + 1) * 128] = y[:chunk] + y[chunk:]
        g_col = jnp.sum(jnp.where(diag128, g_last[c][:, p * 128:(p + 1) * 128], 0.0), axis=1, keepdims=True)
        states[p] = g_col * h + _mm(mb[i], hb) + n0[i]
    for p in range(N_PAIRS):
        h_ref[p] = states[p]

    y = y_ref[...]
    mean = _head_sums(y, bd) * (1.0 / HEAD_DIM)
    yc = y - mean
    var = _head_sums(yc * yc, bd) * (1.0 / HEAD_DIM)
    yn = yc * lax.rsqrt(var + GN_EPS) * lng_ref[...] + lnb_ref[...]
    bonus = _head_sums(r * k * rk_ref[...], bd) * vr
    o_ref[...] = ((yn + bonus) * gate).astype(o_ref.dtype)

    @pl.when(t == nt - 1)
    def _():
        sout_ref[0] = h_ref[...]


def _rwkv(prw, shift0, s0bd, wts, nseq, tr, chunk):
    n = prw.shape[0]
    tiles = n // (nseq * tr)
    row = lambda s, t: (s * tiles + t, 0)
    vec = lambda w: _const_spec((1, w))
    scratch = [pltpu.VMEM((1, RWKV_PROJ), F32), pltpu.VMEM((N_PAIRS, 128, 128), F32),
               pltpu.VMEM((tr, WIDTH), F32)]
    return pl.pallas_call(
        functools.partial(_rwkv_kernel, chunk=chunk),
        grid=(nseq, tiles),
        in_specs=[
            pl.BlockSpec((tr, RWKV_PROJ), row),
            pl.BlockSpec((1, 1, RWKV_PROJ), lambda s, t: (s, 0, 0)),
            pl.BlockSpec((1, N_PAIRS, 128, 128), lambda s, t: (s, 0, 0, 0)),
            vec(RWKV_PROJ), vec(WIDTH), vec(WIDTH),
            _const_spec((128, 2 * WIDTH)), _const_spec((128, WIDTH)),
            vec(WIDTH), vec(WIDTH), vec(WIDTH), vec(WIDTH), vec(WIDTH),
            _const_spec((WIDTH, WIDTH)),
        ],
        out_specs=(
            pl.BlockSpec((tr, WIDTH), row),
            pl.BlockSpec((1, N_PAIRS, 128, 128), lambda s, t: (s, 0, 0, 0)),
        ),
        out_shape=(
            jax.ShapeDtypeStruct((n, WIDTH), BF16),
            jax.ShapeDtypeStruct((nseq, N_PAIRS, 128, 128), F32),
        ),
        scratch_shapes=scratch,
        compiler_params=pltpu.CompilerParams(
            dimension_semantics=("arbitrary", "arbitrary"), vmem_limit_bytes=VMEM_LIMIT_BYTES),
        name="rwkv7_mix",
    )(prw, shift0, s0bd, *wts)


def _prep_rwkv_weights(mu, w0, w2, a0, a2, g2, k_k, k_a, r_k, ln_g, ln_b):
    z = jnp.zeros((64, WIDTH), F32)
    w2a2 = jnp.concatenate([jnp.concatenate([w2, z], axis=1), jnp.concatenate([z, a2], axis=1)], axis=0)
    seg = jnp.arange(WIDTH) // HEAD_DIM
    bd = (seg[:, None] == seg[None, :]).astype(BF16)
    return (mu[None], w0[None], a0[None], w2a2.astype(BF16), g2.astype(BF16), k_k[None], k_a[None],
            r_k.reshape(1, WIDTH), ln_g[None], ln_b[None], bd)


def _state_to_pairs(s):
    n = s.shape[0]
    ht = jnp.swapaxes(s, -1, -2).reshape(n, N_PAIRS, 2, HEAD_DIM, HEAD_DIM)
    z = jnp.zeros_like(ht[:, :, 0])
    top = jnp.concatenate([ht[:, :, 0], z], axis=-1)
    bot = jnp.concatenate([z, ht[:, :, 1]], axis=-1)
    return jnp.concatenate([top, bot], axis=-2)


def _pairs_to_state(hbd):
    n = hbd.shape[0]
    even = hbd[:, :, :HEAD_DIM, :HEAD_DIM]
    odd = hbd[:, :, HEAD_DIM:, HEAD_DIM:]
    ht = jnp.stack([even, odd], axis=2).reshape(n, N_HEADS, HEAD_DIM, HEAD_DIM)
    return jnp.swapaxes(ht, -1, -2)


def _attn_step_kernel(q_ref, kn_ref, vn_ref, lf_ref, ck_ref, cv_ref, clf_ref, o_ref):
    t = q_ref.shape[0]
    past = ck_ref.shape[1]
    mi = lax.broadcasted_iota(jnp.int32, (past, past), 0)
    ji = lax.broadcasted_iota(jnp.int32, (past, past), 1)
    after = jnp.where(mi > ji, 1.0, 0.0).astype(BF16)
    suffix = jnp.zeros(clf_ref.shape[1:], F32)
    for part in _split_bf16(clf_ref[0], N_SPLIT):
        suffix = suffix + jnp.dot(part, after, preferred_element_type=F32)
    suffix = suffix * LOG2E
    ri = lax.broadcasted_iota(jnp.int32, (t, t), 0)
    ci = lax.broadcasted_iota(jnp.int32, (t, t), 1)
    causal = ci <= ri
    tri = jnp.where(causal, 1.0, 0.0).astype(BF16)
    cn = jnp.zeros(lf_ref.shape, F32)
    for part in _split_bf16(lf_ref[...], N_SPLIT):
        cn = cn + jnp.dot(tri, part, preferred_element_type=F32)
    cn = cn * LOG2E
    cn_t = cn.T

    lane = lax.broadcasted_iota(jnp.int32, (t, 128), 1)
    for p in range(N_PAIRS):
        q_pair = q_ref[:, p * PAIR_LANES:p * PAIR_LANES + 128]
        lanes = slice(p * 128, (p + 1) * 128)
        k_past = ck_ref[0, :, lanes].astype(BF16)
        v_past = cv_ref[0, :, lanes].astype(BF16)
        k_new = kn_ref[:, lanes].astype(BF16)
        v_new = vn_ref[:, lanes].astype(BF16)
        outs = []
        for e in range(2):
            hd = 2 * p + e
            mine = (lane >= e * HEAD_DIM) & (lane < (e + 1) * HEAD_DIM)
            qm = jnp.where(mine, q_pair, jnp.zeros_like(q_pair))
            cq = cn[:, hd:hd + 1]
            s_past = lax.dot_general(qm, k_past, (((1,), (1,)), ((), ())), preferred_element_type=F32)
            s_past = s_past + cq + suffix[hd:hd + 1, :]
            s_new = lax.dot_general(qm, k_new, (((1,), (1,)), ((), ())), preferred_element_type=F32)
            s_new = jnp.where(causal, s_new + cq - cn_t[hd:hd + 1, :], NEG_BIG)
            m = jnp.maximum(jnp.max(s_past, axis=1, keepdims=True), jnp.max(s_new, axis=1, keepdims=True))
            p_past = jnp.exp2(s_past - m)
            p_new = jnp.exp2(s_new - m)
            denom = jnp.sum(p_past, axis=1, keepdims=True) + jnp.sum(p_new, axis=1, keepdims=True)
            pv = (jnp.dot(p_past.astype(BF16), v_past, preferred_element_type=F32)
                  + jnp.dot(p_new.astype(BF16), v_new, preferred_element_type=F32))
            outs.append(pv / denom)
        o_ref[:, lanes] = jnp.where(lane < HEAD_DIM, outs[0], outs[1]).astype(o_ref.dtype)


def _attention_step(qaug, kn, vn, logf, cache_k, cache_v, cache_logf_t, t):
    n = qaug.shape[0]
    nseq = n // t
    past = cache_k.shape[1]
    row = lambda b: (b, 0)
    return pl.pallas_call(
        _attn_step_kernel,
        grid=(nseq,),
        in_specs=[
            pl.BlockSpec((t, AUG_WIDTH), row),
            pl.BlockSpec((t, WIDTH), row),
            pl.BlockSpec((t, WIDTH), row),
            pl.BlockSpec((t, 128), row),
            pl.BlockSpec((1, past, WIDTH), lambda b: (b, 0, 0)),
            pl.BlockSpec((1, past, WIDTH), lambda b: (b, 0, 0)),
            pl.BlockSpec((1, N_HEADS, past), lambda b: (b, 0, 0)),
        ],
        out_specs=pl.BlockSpec((t, WIDTH), row),
        out_shape=jax.ShapeDtypeStruct((n, WIDTH), BF16),
        compiler_params=pltpu.CompilerParams(
            dimension_semantics=("arbitrary",), vmem_limit_bytes=VMEM_LIMIT_BYTES),
        name="fox_attention_step",
    )(qaug, kn, vn, logf, cache_k, cache_v, cache_logf_t)


def _out_kernel(x_ref, of_ref, orw_ref, wof_ref, worw_ref, g_ref, wg_ref, wu_ref, wo_ref, o_ref):
    mix = lax.dot_general(of_ref[0], wof_ref[...], (((0,), (0,)), ((), ())), preferred_element_type=F32)
    mix = mix + jnp.dot(orw_ref[...], worw_ref[...], preferred_element_type=F32)
    o_ref[...] = _swiglu_half_step(x_ref[...] + mix, g_ref[...], wg_ref, wu_ref, wo_ref)


def _out_ffn(x, o_fox_t, o_rwkv, wo_fox, wo_rwkv, g, wg, wu, wo, tm):
    n = x.shape[0]
    row = lambda i: (i, 0)
    return pl.pallas_call(
        _out_kernel,
        grid=(n // tm,),
        in_specs=[
            pl.BlockSpec((tm, D_MODEL), row),
            pl.BlockSpec((1, WIDTH, tm), lambda i: (i, 0, 0)),
            pl.BlockSpec((tm, WIDTH), row),
            _const_spec((WIDTH, D_MODEL)),
            _const_spec((WIDTH, D_MODEL)),
            _const_spec((1, D_MODEL)),
            _const_spec((D_MODEL, D_FF)),
            _const_spec((D_MODEL, D_FF)),
            _const_spec((D_FF, D_MODEL)),
        ],
        out_specs=pl.BlockSpec((tm, D_MODEL), row),
        out_shape=jax.ShapeDtypeStruct((n, D_MODEL), F32),
        compiler_params=pltpu.CompilerParams(
            dimension_semantics=("arbitrary",), vmem_limit_bytes=VMEM_LIMIT_BYTES),
        name="out_proj_ffn",
    )(x, o_fox_t, o_rwkv, wo_fox, wo_rwkv, g, wg, wu, wo)


PROMPT_TILE = 512
RWKV_TILE = 256
RWKV_CHUNK = 64


def kernel(x_prompt, x_sample, cache_fox_k, cache_fox_v, cache_fox_logf, state_rwkv, state_rwkv_shift, norm_ffn1_g, ffn1_w_in, ffn1_w_out, norm_mix_g, w_in, w_out, fox_b_f, fox_q_norm_g, fox_k_norm_g, rwkv_mu, rwkv_w0, rwkv_w2, rwkv_a0, rwkv_a2, rwkv_g2, rwkv_k_k, rwkv_k_a, rwkv_r_k, rwkv_ln_g, rwkv_ln_b, norm_ffn2_g, ffn2_w_in, ffn2_w_out):
    depth = norm_ffn1_g.shape[0]
    bp, seq, _ = x_prompt.shape
    bs, dec_seq, _ = x_sample.shape
    assert bp == 1, "the prompt path assumes a single stream"
    n_p, n_s = bp * seq, bs * dec_seq
    yp = x_prompt.reshape(n_p, D_MODEL)
    ys = x_sample.reshape(n_s, D_MODEL)
    prompt_states = ([], [], [], [], [])
    sample_states = ([], [], [], [], [])
    for l in range(depth):
        ffn1 = (norm_ffn1_g[l][None], ffn1_w_in[l, :, :D_FF].astype(BF16), ffn1_w_in[l, :, D_FF:].astype(BF16),
                ffn1_w_out[l].astype(BF16))
        ffn2 = (norm_ffn2_g[l][None], ffn2_w_in[l, :, :D_FF].astype(BF16), ffn2_w_in[l, :, D_FF:].astype(BF16),
                ffn2_w_out[l].astype(BF16))
        proj_w = _prep_proj_weights(w_in[l], fox_b_f[l], fox_q_norm_g[l], fox_k_norm_g[l])
        rwkv_w = _prep_rwkv_weights(rwkv_mu[l], rwkv_w0[l], rwkv_w2[l], rwkv_a0[l], rwkv_a2[l], rwkv_g2[l],
                                    rwkv_k_k[l], rwkv_k_a[l], rwkv_r_k[l], rwkv_ln_g[l], rwkv_ln_b[l])
        wo_fox = w_out[l, :WIDTH].astype(BF16)
        wo_rwkv = w_out[l, WIDTH:].astype(BF16)
        mix_g = norm_mix_g[l][None]

        x1 = _ffn(yp, *ffn1, PROMPT_TILE)
        qaug, kn, kaug, v, vt, logf, prw, base = _proj(x1, mix_g, *proj_w, PROMPT_TILE)
        o_fox_t = _attention(base[:, 0, :N_HEADS].T, qaug, kaug, vt, PROMPT_TILE)
        o_rwkv, s_pairs = _rwkv(prw, jnp.zeros((bp, 1, RWKV_PROJ), F32),
                                jnp.zeros((bp, N_PAIRS, 128, 128), F32), rwkv_w, bp, RWKV_TILE, RWKV_CHUNK)
        yp = _out_ffn(x1, o_fox_t, o_rwkv, wo_fox, wo_rwkv, *ffn2, PROMPT_TILE)
        for acc, st in zip(prompt_states, (
                kn.reshape(bp, seq, N_HEADS, HEAD_DIM), v.reshape(bp, seq, N_HEADS, HEAD_DIM),
                logf[:, :N_HEADS].reshape(bp, seq, N_HEADS), _pairs_to_state(s_pairs),
                prw.reshape(bp, seq, RWKV_PROJ)[:, -1:])):
            acc.append(st)

        x1 = _ffn(ys, *ffn1, n_s)
        qaug, kn, kaug, v, vt, logf, prw, base = _proj(x1, mix_g, *proj_w, n_s)
        past = cache_fox_k.shape[2]
        o_fox = _attention_step(
            qaug, kn, v, logf, cache_fox_k[l].reshape(bs, past, WIDTH), cache_fox_v[l].reshape(bs, past, WIDTH),
            jnp.swapaxes(cache_fox_logf[l], 1, 2), dec_seq)
        o_rwkv, s_pairs = _rwkv(prw, state_rwkv_shift[l], _state_to_pairs(state_rwkv[l]), rwkv_w,
                                bs, dec_seq, dec_seq)
        ys = _out_ffn(x1, o_fox.T[None], o_rwkv, wo_fox, wo_rwkv, *ffn2, n_s)
        for acc, st in zip(sample_states, (
                kn.reshape(bs, dec_seq, N_HEADS, HEAD_DIM), v.reshape(bs, dec_seq, N_HEADS, HEAD_DIM),
                logf[:, :N_HEADS].reshape(bs, dec_seq, N_HEADS), _pairs_to_state(s_pairs),
                prw.reshape(bs, dec_seq, RWKV_PROJ)[:, -1:])):
            acc.append(st)

    return (yp.reshape(bp, seq, D_MODEL), ys.reshape(bs, dec_seq, D_MODEL),
            *(jnp.stack(a) for a in prompt_states), *(jnp.stack(a) for a in sample_states))
```

```python
import functools
import math

import jax
import jax.numpy as jnp
from jax import lax
from jax.experimental import pallas as pl
from jax.experimental.pallas import tpu as pltpu

F32 = jnp.float32
BF16 = jnp.bfloat16

D_MODEL = 1024
D_FF = 2816
HEAD_DIM = 64
N_HEADS = 8
WIDTH = N_HEADS * HEAD_DIM
RWKV_PROJ = 1792
RMS_EPS = 1e-6
GN_EPS = 64e-5
LOG2E = math.log2(math.e)

VMEM_LIMIT_BYTES = 56 * 1024 * 1024
FF_CHUNK = 512


def _const_spec(shape):
    return pl.BlockSpec(shape, lambda *_: (0,) * len(shape), pipeline_mode=pl.Buffered(1))


def _rms(x, g):
    return x * lax.rsqrt(jnp.mean(x * x, axis=-1, keepdims=True) + RMS_EPS) * g


def _swiglu_half_step(x, g, wg_ref, wu_ref, wo_ref):
    h = _rms(x, g).astype(BF16)
    acc = x
    for c0 in range(0, D_FF, FF_CHUNK):
        c1 = min(c0 + FF_CHUNK, D_FF)
        gate = jnp.dot(h, wg_ref[:, c0:c1], preferred_element_type=F32)
        up = jnp.dot(h, wu_ref[:, c0:c1], preferred_element_type=F32)
        a = (gate * jax.nn.sigmoid(gate) * up).astype(BF16)
        acc = acc + 0.5 * jnp.dot(a, wo_ref[c0:c1, :], preferred_element_type=F32)
    return acc


def _ffn_kernel(x_ref, g_ref, wg_ref, wu_ref, wo_ref, o_ref):
    o_ref[...] = _swiglu_half_step(x_ref[...], g_ref[...], wg_ref, wu_ref, wo_ref)


def _ffn(x, g, wg, wu, wo, tm):
    n = x.shape[0]
    return pl.pallas_call(
        _ffn_kernel,
        grid=(n // tm,),
        in_specs=[
            pl.BlockSpec((tm, D_MODEL), lambda i: (i, 0)),
            _const_spec((1, D_MODEL)),
            _const_spec((D_MODEL, D_FF)),
            _const_spec((D_MODEL, D_FF)),
            _const_spec((D_FF, D_MODEL)),
        ],
        out_specs=pl.BlockSpec((tm, D_MODEL), lambda i: (i, 0)),
        out_shape=jax.ShapeDtypeStruct((n, D_MODEL), F32),
        compiler_params=pltpu.CompilerParams(
            dimension_semantics=("arbitrary",), vmem_limit_bytes=VMEM_LIMIT_BYTES),
        name="ffn",
    )(x, g, wg, wu, wo)


PROJ_COLS = 3 * WIDTH + RWKV_PROJ + 128
V_ROWS = HEAD_DIM + 16
N_SPLIT = 3
PAIR_LANES = 256
AUG_WIDTH = (N_HEADS // 2) * PAIR_LANES


def _split_bf16(x, n):
    parts = []
    for _ in range(n):
        p = x.astype(BF16)
        parts.append(p)
        x = x - p.astype(F32)
    return parts


def _head_sums(x, bd, terms=2):
    return sum(jnp.dot(part, bd, preferred_element_type=F32) for part in _split_bf16(x, terms))


def _log_sigmoid(x):
    return jnp.minimum(x, 0.0) - jnp.log1p(jnp.exp(-jnp.abs(x)))


def _proj_kernel(x_ref, g_ref, w_ref, bf_ref, qg_ref, kg_ref, bd_ref, eq_ref, ek_ref, oneq_ref, onek_ref,
                 qaug_ref, k_ref, kaug_ref, v_ref, vt_ref, logf_ref, prw_ref, base_ref, carry_ref):
    i = pl.program_id(0)
    tm = x_ref.shape[0]

    @pl.when(i == 0)
    def _():
        carry_ref[...] = jnp.zeros_like(carry_ref)

    h = _rms(x_ref[...], g_ref[...]).astype(BF16)
    proj = jnp.dot(h, w_ref[...], preferred_element_type=F32)
    q = proj[:, :WIDTH]
    k = proj[:, WIDTH:2 * WIDTH]
    v = proj[:, 2 * WIDTH:3 * WIDTH]
    prw_ref[...] = proj[:, 3 * WIDTH:3 * WIDTH + RWKV_PROJ]
    f = proj[:, 3 * WIDTH + RWKV_PROJ:]

    bd = bd_ref[...]
    qn = q * lax.rsqrt(_head_sums(q * q, bd, 1) * (1.0 / HEAD_DIM) + RMS_EPS) * qg_ref[...]
    kn = k * lax.rsqrt(_head_sums(k * k, bd, 1) * (1.0 / HEAD_DIM) + RMS_EPS) * kg_ref[...]
    k_ref[...] = kn
    v_ref[...] = v
    v_t = v.T.astype(BF16)
    for hd in range(N_HEADS):
        vt_ref[0, hd * V_ROWS:hd * V_ROWS + HEAD_DIM, :] = v_t[hd * HEAD_DIM:(hd + 1) * HEAD_DIM, :]
        vt_ref[0, hd * V_ROWS + HEAD_DIM:(hd + 1) * V_ROWS, :] = jnp.ones((V_ROWS - HEAD_DIM, tm), BF16)

    lane = lax.broadcasted_iota(jnp.int32, f.shape, 1)
    logf = jnp.where(lane < N_HEADS, _log_sigmoid(f + bf_ref[...]), 0.0)
    logf_ref[...] = logf

    row = lax.broadcasted_iota(jnp.int32, (tm, tm), 0)
    col = lax.broadcasted_iota(jnp.int32, (tm, tm), 1)
    tri = jnp.where(col <= row, 1.0, 0.0).astype(BF16)
    cl = jnp.zeros_like(logf)
    for part in _split_bf16(logf, N_SPLIT):
        cl = cl + jnp.dot(tri, part, preferred_element_type=F32)
    c2 = cl * LOG2E
    base_ref[0] = jnp.broadcast_to(carry_ref[...], base_ref.shape[1:])
    carry_ref[...] = carry_ref[...] + c2[tm - 1:tm, :]

    parts = jnp.concatenate(_split_bf16(c2, N_SPLIT), axis=1)
    aug_q = (jnp.dot(parts, eq_ref[...], preferred_element_type=F32) + oneq_ref[...]).astype(BF16)
    aug_k = (jnp.dot(parts, ek_ref[...], preferred_element_type=F32) + onek_ref[...]).astype(BF16)
    qs = (qn * (HEAD_DIM ** -0.5 * LOG2E)).astype(BF16)
    ks = kn.astype(BF16)
    for p in range(N_HEADS // 2):
        qaug_ref[:, p * PAIR_LANES:p * PAIR_LANES + 128] = qs[:, p * 128:(p + 1) * 128]
        qaug_ref[:, p * PAIR_LANES + 128:(p + 1) * PAIR_LANES] = aug_q[:, p * 128:(p + 1) * 128]
        kaug_ref[:, p * PAIR_LANES:p * PAIR_LANES + 128] = ks[:, p * 128:(p + 1) * 128]
        kaug_ref[:, p * PAIR_LANES + 128:(p + 1) * PAIR_LANES] = aug_k[:, p * 128:(p + 1) * 128]


def _carrier_constants():
    eq = [[0.0] * WIDTH for _ in range(N_SPLIT * 128)]
    ek = [[0.0] * WIDTH for _ in range(N_SPLIT * 128)]
    oneq = [0.0] * WIDTH
    onek = [0.0] * WIDTH
    for hd in range(N_HEADS):
        off = (hd // 2) * 128 + (hd % 2) * 8
        for s in range(N_SPLIT):
            eq[s * 128 + hd][off + s] = 1.0
            ek[s * 128 + hd][off + N_SPLIT + s] = -1.0
            oneq[off + N_SPLIT + s] = 1.0
            onek[off + s] = 1.0
    return (jnp.array(eq, BF16), jnp.array(ek, BF16), jnp.array([oneq], F32), jnp.array([onek], F32))


def _prep_proj_weights(w_in, fox_b_f, q_norm_g, k_norm_g):
    fox_cols = 3 * WIDTH
    w_f = jnp.pad(w_in[:, fox_cols:fox_cols + N_HEADS], ((0, 0), (0, 128 - N_HEADS)))
    w_all = jnp.concatenate([w_in[:, :fox_cols], w_in[:, fox_cols + N_HEADS:], w_f], axis=1).astype(BF16)
    bf_pad = jnp.pad(fox_b_f, (0, 128 - N_HEADS))[None]
    return w_all, bf_pad, jnp.tile(q_norm_g, N_HEADS)[None], jnp.tile(k_norm_g, N_HEADS)[None]


def _proj(x, g, w_all, bf_pad, qg, kg, tm):
    n = x.shape[0]
    nblk = n // tm
    seg = jnp.arange(WIDTH) // HEAD_DIM
    bd = (seg[:, None] == seg[None, :]).astype(BF16)
    eq, ek, oneq, onek = _carrier_constants()
    row = lambda i: (i, 0)
    out_shape = (
        jax.ShapeDtypeStruct((n, AUG_WIDTH), BF16),
        jax.ShapeDtypeStruct((n, WIDTH), F32),
        jax.ShapeDtypeStruct((n, AUG_WIDTH), BF16),
        jax.ShapeDtypeStruct((n, WIDTH), F32),
        jax.ShapeDtypeStruct((nblk, N_HEADS * V_ROWS, tm), BF16),
        jax.ShapeDtypeStruct((n, 128), F32),
        jax.ShapeDtypeStruct((n, RWKV_PROJ), F32),
        jax.ShapeDtypeStruct((nblk, 8, 128), F32),
    )
    out_specs = (
        pl.BlockSpec((tm, AUG_WIDTH), row),
        pl.BlockSpec((tm, WIDTH), row),
        pl.BlockSpec((tm, AUG_WIDTH), row),
        pl.BlockSpec((tm, WIDTH), row),
        pl.BlockSpec((1, N_HEADS * V_ROWS, tm), lambda i: (i, 0, 0)),
        pl.BlockSpec((tm, 128), row),
        pl.BlockSpec((tm, RWKV_PROJ), row),
        pl.BlockSpec((1, 8, 128), lambda i: (i, 0, 0)),
    )
    return pl.pallas_call(
        _proj_kernel,
        grid=(nblk,),
        in_specs=[
            pl.BlockSpec((tm, D_MODEL), row),
            _const_spec((1, D_MODEL)),
            _const_spec((D_MODEL, PROJ_COLS)),
            _const_spec((1, 128)),
            _const_spec((1, WIDTH)),
            _const_spec((1, WIDTH)),
            _const_spec((WIDTH, WIDTH)),
            _const_spec((N_SPLIT * 128, WIDTH)),
            _const_spec((N_SPLIT * 128, WIDTH)),
            _const_spec((1, WIDTH)),
            _const_spec((1, WIDTH)),
        ],
        out_specs=out_specs,
        out_shape=out_shape,
        scratch_shapes=[pltpu.VMEM((1, 128), F32)],
        compiler_params=pltpu.CompilerParams(
            dimension_semantics=("arbitrary",), vmem_limit_bytes=VMEM_LIMIT_BYTES),
        name="mix_proj",
    )(x, g, w_all, bf_pad, qg, kg, bd, eq, ek, oneq, onek)


NEG_BIG = -1e30
STRIP = 256


def _attn_kernel(base_ref, q_ref, k_ref, vt_ref, o_ref,
                 qt_ref, s0_ref, s1_ref, x0_ref, x1_ref, p0_ref, p1_ref, m_ref, alpha_ref, acc_ref, *, tk):
    pair = pl.program_id(0)
    qi = pl.program_id(1)
    tq = q_ref.shape[0]
    sub = tq // tk
    n_steps = (qi + 1) * sub
    assert sub in (1, 2), "the first (peeled) trip must hold every diagonal block"
    strips = [(e, h) for e in range(2) for h in range(tq // STRIP)]

    def cols(e, h):
        return slice(e * tq + h * STRIP, e * tq + (h + 1) * STRIP)

    lane = lax.broadcasted_iota(jnp.int32, (tq, PAIR_LANES), 1)
    q = q_ref[...].astype(F32)
    for e in range(2):
        f0 = e * HEAD_DIM
        c0 = 128 + e * 8
        keep = ((lane >= f0) & (lane < f0 + HEAD_DIM)) | ((lane >= c0) & (lane < c0 + 2 * N_SPLIT))
        qt_ref[:, e * tq:(e + 1) * tq] = jnp.where(keep, q, 0.0).T.astype(BF16)

    def block_of(step):
        return jnp.where(step < sub, qi * sub + step, step - sub)

    def offset(step, e, h):
        hd = 2 * pair + e
        d = base_ref[hd, qi * sub + (h * STRIP) // tk] - base_ref[hd, block_of(step)]
        return jnp.where(step < n_steps, d, NEG_BIG)

    r = lax.broadcasted_iota(jnp.int32, (tk, STRIP), 0)
    c = lax.broadcasted_iota(jnp.int32, (tk, STRIP), 1)

    def scores(step, s_ref, smax_ref, e, h, diag=None):
        blk = block_of(jnp.minimum(step, n_steps))
        kc = k_ref[pl.ds(pl.multiple_of(blk * tk, tk), tk), :]
        s = jnp.dot(kc, qt_ref[:, cols(e, h)], preferred_element_type=F32)
        if diag is not None:
            s = jnp.where(r + diag * tk <= c + h * STRIP, s, NEG_BIG)
        s_ref[:, cols(e, h)] = s
        smax_ref[:, cols(e, h)] = jnp.max(s, axis=0, keepdims=True)

    def softmax(step, s_ref, smax_ref, p_ref):
        for e, h in strips:
            cs = cols(e, h)
            d = offset(step, e, h)
            m_old = m_ref[:, cs]
            m_new = jnp.maximum(m_old, smax_ref[:, cs] + d)
            m_ref[:, cs] = m_new
            alpha_ref[:, cs] = jnp.exp2(m_old - m_new)
            p_ref[:, cs] = jnp.exp2((s_ref[:, cs] - (m_new - d)).astype(BF16))

    def accumulate(step, p_ref, e, h):
        vt = vt_ref[block_of(step), e * V_ROWS:(e + 1) * V_ROWS, :]
        pv = jnp.dot(vt, p_ref[:, cols(e, h)], preferred_element_type=F32)
        hs = slice(h * STRIP, (h + 1) * STRIP)
        acc_ref[e, :, hs] = alpha_ref[:, cols(e, h)] * acc_ref[e, :, hs] + pv

    def two_steps(k, first=False):
        for e, h in strips:
            if not first:
                accumulate(k - 1, p1_ref, e, h)
            scores(k + 1, s1_ref, x1_ref, e, h, diag=1 if (first and sub > 1) else None)
        softmax(k, s0_ref, x0_ref, p0_ref)
        for e, h in strips:
            accumulate(k, p0_ref, e, h)
            scores(k + 2, s0_ref, x0_ref, e, h)
        softmax(k + 1, s1_ref, x1_ref, p1_ref)

    m_ref[...] = jnp.full(m_ref.shape, NEG_BIG, F32)
    acc_ref[...] = jnp.zeros_like(acc_ref)
    for e, h in strips:
        scores(0, s0_ref, x0_ref, e, h, diag=0)
    two_steps(0, first=True)

    def trip(it, carry):
        two_steps(2 * it)
        return carry

    n_trips = (n_steps + 1) // 2
    lax.fori_loop(1, n_trips, trip, 0)
    for e, h in strips:
        accumulate(2 * n_trips - 1, p1_ref, e, h)
    for e in range(2):
        acc = acc_ref[e]
        o_ref[0, e * HEAD_DIM:(e + 1) * HEAD_DIM, :] = (
            acc[:HEAD_DIM] / acc[HEAD_DIM:HEAD_DIM + 1]).astype(o_ref.dtype)


def _attention(base, qaug, kaug, vt, tq, tk):
    n = qaug.shape[0]
    grid_spec = pltpu.PrefetchScalarGridSpec(
        num_scalar_prefetch=1,
        grid=(N_HEADS // 2, n // tq),
        in_specs=[
            pl.BlockSpec((tq, PAIR_LANES), lambda p, i, b: (i, p)),
            pl.BlockSpec((n, PAIR_LANES), lambda p, i, b: (0, p)),
            pl.BlockSpec((n // tk, 2 * V_ROWS, tk), lambda p, i, b: (0, p, 0)),
        ],
        out_specs=pl.BlockSpec((1, 2 * HEAD_DIM, tq), lambda p, i, b: (i, p, 0)),
        scratch_shapes=[
            pltpu.VMEM((PAIR_LANES, 2 * tq), BF16),
            pltpu.VMEM((tk, 2 * tq), F32),
            pltpu.VMEM((tk, 2 * tq), F32),
            pltpu.VMEM((1, 2 * tq), F32),
            pltpu.VMEM((1, 2 * tq), F32),
            pltpu.VMEM((tk, 2 * tq), BF16),
            pltpu.VMEM((tk, 2 * tq), BF16),
            pltpu.VMEM((1, 2 * tq), F32),
            pltpu.VMEM((1, 2 * tq), F32),
            pltpu.VMEM((2, V_ROWS, tq), F32),
        ],
    )
    return pl.pallas_call(
        functools.partial(_attn_kernel, tk=tk),
        grid_spec=grid_spec,
        out_shape=jax.ShapeDtypeStruct((n // tq, WIDTH, tq), BF16),
        compiler_params=pltpu.CompilerParams(
            dimension_semantics=("arbitrary", "arbitrary"), vmem_limit_bytes=VMEM_LIMIT_BYTES),
        name="fox_attention",
    )(base, qaug, kaug, vt)


N_PAIRS = N_HEADS // 2


def _mm(a, b):
    return jnp.dot(a.astype(BF16), b.astype(BF16), preferred_element_type=F32)


def _mm_nt(a, b):
    return lax.dot_general(a.astype(BF16), b.astype(BF16), (((1,), (1,)), ((), ())), preferred_element_type=F32)


def _mm_tn(a, b):
    return lax.dot_general(a.astype(BF16), b.astype(BF16), (((0,), (0,)), ((), ())), preferred_element_type=F32)


def _rwkv_kernel(prw_ref, shift0_ref, s0_ref, mu_ref, w0_ref, a0_ref, w2a2_ref, g2_ref, kk_ref, ka_ref, rk_ref,
                 lng_ref, lnb_ref, bd_ref,
                 o_ref, sout_ref,
                 prev_ref, h_ref, y_ref,
                 *, chunk):
    t = pl.program_id(1)
    nt = pl.num_programs(1)
    tr = prw_ref.shape[0]
    n_chunks = tr // chunk

    @pl.when(t == 0)
    def _():
        prev_ref[...] = shift0_ref[0]
        h_ref[...] = s0_ref[0]

    prw = prw_ref[...]
    rolled = pltpu.roll(prw, 1, axis=0)
    row = lax.broadcasted_iota(jnp.int32, prw.shape, 0)
    prev = jnp.where(row == 0, prev_ref[...], rolled)
    prev_ref[...] = prw[tr - 1:tr, :]
    xs = prw + (prev - prw) * mu_ref[...]

    r = xs[:, :WIDTH]
    kr = xs[:, WIDTH:2 * WIDTH]
    vr = xs[:, 2 * WIDTH:3 * WIDTH]
    wa = xs[:, 3 * WIDTH:3 * WIDTH + 128]
    gd = xs[:, 3 * WIDTH + 128:]
    lane128 = lax.broadcasted_iota(jnp.int32, wa.shape, 1)
    wa_in = jnp.where(lane128 < 64, jnp.tanh(wa), wa).astype(BF16)
    lora = jnp.dot(wa_in, w2a2_ref[...], preferred_element_type=F32)
    w_log = _log_sigmoid(w0_ref[...] + lora[:, :WIDTH]) - 0.5
    lw = -jnp.exp(w_log)
    lr = jax.nn.sigmoid(a0_ref[...] + lora[:, WIDTH:])
    gate = jnp.dot(jax.nn.sigmoid(gd).astype(BF16), g2_ref[...], preferred_element_type=F32)

    bd = bd_ref[...]
    kk = kr * kk_ref[...]
    kk = kk / jnp.maximum(jnp.sqrt(_head_sums(kk * kk, bd)), 1e-12)
    k = kr * (1.0 + (lr - 1.0) * ka_ref[...])
    b = kk * lr

    ri = lax.broadcasted_iota(jnp.int32, (tr, tr), 0)
    ci = lax.broadcasted_iota(jnp.int32, (tr, tr), 1)
    tri = jnp.where((ci <= ri) & (ci // chunk == ri // chunk), 1.0, 0.0).astype(BF16)
    cs = jnp.zeros_like(lw)
    for part in _split_bf16(lw, N_SPLIT):
        cs = cs + jnp.dot(tri, part, preferred_element_type=F32)
    e_pos = jnp.exp(cs)
    e_neg = jnp.exp(-cs)
    at = (-kk * jnp.exp(cs - lw)).astype(BF16)
    rt = (r * e_pos).astype(BF16)
    bt = (b * e_neg).astype(BF16)
    kt = (k * e_neg).astype(BF16)
    vb = vr.astype(BF16)
    g_last, bh_rows, kh_rows = [], [], []
    for c in range(n_chunks):
        rows = slice(c * chunk, (c + 1) * chunk)
        last = cs[(c + 1) * chunk - 1:(c + 1) * chunk, :]
        to_end = jnp.exp(last - cs[rows, :])
        bh_rows.append((b[rows, :] * to_end).astype(BF16))
        kh_rows.append((k[rows, :] * to_end).astype(BF16))
        g_last.append(jnp.exp(last))

    c2 = 2 * chunk
    lane = lax.broadcasted_iota(jnp.int32, (chunk, 128), 1)
    even = lane < HEAD_DIM

    def stack(x, c, p):
        x = x[c * chunk:(c + 1) * chunk, p * 128:(p + 1) * 128]
        zero = jnp.zeros_like(x)
        return jnp.concatenate([jnp.where(even, x, zero), jnp.where(even, zero, x)], axis=0)

    rr = lax.broadcasted_iota(jnp.int32, (c2, c2), 0)
    cc = lax.broadcasted_iota(jnp.int32, (c2, c2), 1)
    same = (rr // chunk) == (cc // chunk)
    strict = same & (cc < rr)
    incl = same & (cc <= rr)
    eye = jnp.where(rr == cc, 1.0, 0.0)
    r128 = lax.broadcasted_iota(jnp.int32, (128, 128), 0)
    c128 = lax.broadcasted_iota(jnp.int32, (128, 128), 1)
    diag128 = r128 == c128

    insts = [(c, p) for c in range(n_chunks) for p in range(N_PAIRS)]
    a_s = [stack(at, c, p) for c, p in insts]
    r_s = [stack(rt, c, p) for c, p in insts]
    b_s = [stack(bt, c, p) for c, p in insts]
    k_s = [stack(kt, c, p) for c, p in insts]
    v_s = [stack(vb, c, p) for c, p in insts]
    bh_s = [stack(bh_rows[c], 0, p) for c, p in insts]
    kh_s = [stack(kh_rows[c], 0, p) for c, p in insts]
    ar_s = [jnp.concatenate([a, rr_], axis=0) for a, rr_ in zip(a_s, r_s)]
    gb = [_mm_nt(x, y) for x, y in zip(ar_s, b_s)]
    gk = [_mm_nt(x, y) for x, y in zip(ar_s, k_s)]
    a_ab = [jnp.where(strict, g[:c2], 0.0) for g in gb]
    a_ak = [jnp.where(strict, g[:c2], 0.0).astype(BF16) for g in gk]
    a_rb = [jnp.where(incl, g[c2:], 0.0).astype(BF16) for g in gb]
    a_rk = [jnp.where(incl, g[c2:], 0.0).astype(BF16) for g in gk]
    inv = [eye + x for x in a_ab]
    pw = [x.astype(BF16) for x in a_ab]
    for _ in range(int(math.log2(chunk)) - 1):
        pw = [_mm(x, x).astype(BF16) for x in pw]
        inv = [i + _mm(i, x) for i, x in zip(inv, pw)]
    inv = [i.astype(BF16) for i in inv]
    x1 = [_mm(m, v) for m, v in zip(a_ak, v_s)]
    wm = [_mm(i, a).astype(BF16) for i, a in zip(inv, a_s)]
    u0 = [_mm(i, x).astype(BF16) for i, x in zip(inv, x1)]
    mb = [_mm_tn(bh, w).astype(BF16) for bh, w in zip(bh_s, wm)]
    rm = [(rr_.astype(F32) + _mm(m, w)).astype(BF16) for rr_, m, w in zip(r_s, a_rb, wm)]
    y0 = [_mm(m1, u) + _mm(m2, v) for m1, u, m2, v in zip(a_rb, u0, a_rk, v_s)]
    n0 = [_mm_tn(bh, u) + _mm_tn(kh, v) for bh, u, kh, v in zip(bh_s, u0, kh_s, v_s)]

    states = [h_ref[p] for p in range(N_PAIRS)]
    for i, (c, p) in enumerate(insts):
        h = states[p]
        hb = h.astype(BF16)
        y = _mm(rm[i], hb) + y0[i]
        y_ref[c * chunk:(c + 1) * chunk, p * 128:(p + 1) * 128] = y[:chunk] + y[chunk:]
        g_col = jnp.sum(jnp.where(diag128, g_last[c][:, p * 128:(p + 1) * 128], 0.0), axis=1, keepdims=True)
        states[p] = g_col * h + _mm(mb[i], hb) + n0[i]
    for p in range(N_PAIRS):
        h_ref[p] = states[p]

    y = y_ref[...]
    mean = _head_sums(y, bd, 1) * (1.0 / HEAD_DIM)
    yc = y - mean
    var = _head_sums(yc * yc, bd, 1) * (1.0 / HEAD_DIM)
    yn = yc * lax.rsqrt(var + GN_EPS) * lng_ref[...] + lnb_ref[...]
    bonus = _head_sums(r * k * rk_ref[...], bd, 1) * vr
    o_ref[...] = ((yn + bonus) * gate).astype(o_ref.dtype)

    @pl.when(t == nt - 1)
    def _():
        sout_ref[0] = h_ref[...]


def _rwkv(prw, shift0, s0bd, wts, nseq, tr, chunk):
    n = prw.shape[0]
    tiles = n // (nseq * tr)
    row = lambda s, t: (s * tiles + t, 0)
    vec = lambda w: _const_spec((1, w))
    scratch = [pltpu.VMEM((1, RWKV_PROJ), F32), pltpu.VMEM((N_PAIRS, 128, 128), F32),
               pltpu.VMEM((tr, WIDTH), F32)]
    return pl.pallas_call(
        functools.partial(_rwkv_kernel, chunk=chunk),
        grid=(nseq, tiles),
        in_specs=[
            pl.BlockSpec((tr, RWKV_PROJ), row),
            pl.BlockSpec((1, 1, RWKV_PROJ), lambda s, t: (s, 0, 0)),
            pl.BlockSpec((1, N_PAIRS, 128, 128), lambda s, t: (s, 0, 0, 0)),
            vec(RWKV_PROJ), vec(WIDTH), vec(WIDTH),
            _const_spec((128, 2 * WIDTH)), _const_spec((128, WIDTH)),
            vec(WIDTH), vec(WIDTH), vec(WIDTH), vec(WIDTH), vec(WIDTH),
            _const_spec((WIDTH, WIDTH)),
        ],
        out_specs=(
            pl.BlockSpec((tr, WIDTH), row),
            pl.BlockSpec((1, N_PAIRS, 128, 128), lambda s, t: (s, 0, 0, 0)),
        ),
        out_shape=(
            jax.ShapeDtypeStruct((n, WIDTH), BF16),
            jax.ShapeDtypeStruct((nseq, N_PAIRS, 128, 128), F32),
        ),
        scratch_shapes=scratch,
        compiler_params=pltpu.CompilerParams(
            dimension_semantics=("arbitrary", "arbitrary"), vmem_limit_bytes=VMEM_LIMIT_BYTES),
        name="rwkv7_mix",
    )(prw, shift0, s0bd, *wts)


def _prep_rwkv_weights(mu, w0, w2, a0, a2, g2, k_k, k_a, r_k, ln_g, ln_b):
    z = jnp.zeros((64, WIDTH), F32)
    w2a2 = jnp.concatenate([jnp.concatenate([w2, z], axis=1), jnp.concatenate([z, a2], axis=1)], axis=0)
    seg = jnp.arange(WIDTH) // HEAD_DIM
    bd = (seg[:, None] == seg[None, :]).astype(BF16)
    return (mu[None], w0[None], a0[None], w2a2.astype(BF16), g2.astype(BF16), k_k[None], k_a[None],
            r_k.reshape(1, WIDTH), ln_g[None], ln_b[None], bd)


def _state_to_pairs(s):
    n = s.shape[0]
    ht = jnp.swapaxes(s, -1, -2).reshape(n, N_PAIRS, 2, HEAD_DIM, HEAD_DIM)
    z = jnp.zeros_like(ht[:, :, 0])
    top = jnp.concatenate([ht[:, :, 0], z], axis=-1)
    bot = jnp.concatenate([z, ht[:, :, 1]], axis=-1)
    return jnp.concatenate([top, bot], axis=-2)


def _pairs_to_state(hbd):
    n = hbd.shape[0]
    even = hbd[:, :, :HEAD_DIM, :HEAD_DIM]
    odd = hbd[:, :, HEAD_DIM:, HEAD_DIM:]
    ht = jnp.stack([even, odd], axis=2).reshape(n, N_HEADS, HEAD_DIM, HEAD_DIM)
    return jnp.swapaxes(ht, -1, -2)


def _attn_step_kernel(q_ref, kn_ref, vn_ref, lf_ref, ck_ref, cv_ref, clf_ref, o_ref):
    t = q_ref.shape[0]
    past = ck_ref.shape[1]
    mi = lax.broadcasted_iota(jnp.int32, (past, past), 0)
    ji = lax.broadcasted_iota(jnp.int32, (past, past), 1)
    after = jnp.where(mi > ji, 1.0, 0.0).astype(BF16)
    suffix = jnp.zeros(clf_ref.shape[1:], F32)
    for part in _split_bf16(clf_ref[0], N_SPLIT):
        suffix = suffix + jnp.dot(part, after, preferred_element_type=F32)
    suffix = suffix * LOG2E
    ri = lax.broadcasted_iota(jnp.int32, (t, t), 0)
    ci = lax.broadcasted_iota(jnp.int32, (t, t), 1)
    causal = ci <= ri
    tri = jnp.where(causal, 1.0, 0.0).astype(BF16)
    cn = jnp.zeros(lf_ref.shape, F32)
    for part in _split_bf16(lf_ref[...], N_SPLIT):
        cn = cn + jnp.dot(tri, part, preferred_element_type=F32)
    cn = cn * LOG2E
    cn_t = cn.T

    lane = lax.broadcasted_iota(jnp.int32, (t, 128), 1)
    for p in range(N_PAIRS):
        q_pair = q_ref[:, p * PAIR_LANES:p * PAIR_LANES + 128]
        lanes = slice(p * 128, (p + 1) * 128)
        k_past = ck_ref[0, :, lanes].astype(BF16)
        v_past = cv_ref[0, :, lanes].astype(BF16)
        k_new = kn_ref[:, lanes].astype(BF16)
        v_new = vn_ref[:, lanes].astype(BF16)
        outs = []
        for e in range(2):
            hd = 2 * p + e
            mine = (lane >= e * HEAD_DIM) & (lane < (e + 1) * HEAD_DIM)
            qm = jnp.where(mine, q_pair, jnp.zeros_like(q_pair))
            cq = cn[:, hd:hd + 1]
            s_past = lax.dot_general(qm, k_past, (((1,), (1,)), ((), ())), preferred_element_type=F32)
            s_past = s_past + cq + suffix[hd:hd + 1, :]
            s_new = lax.dot_general(qm, k_new, (((1,), (1,)), ((), ())), preferred_element_type=F32)
            s_new = jnp.where(causal, s_new + cq - cn_t[hd:hd + 1, :], NEG_BIG)
            m = jnp.maximum(jnp.max(s_past, axis=1, keepdims=True), jnp.max(s_new, axis=1, keepdims=True))
            p_past = jnp.exp2(s_past - m)
            p_new = jnp.exp2(s_new - m)
            denom = jnp.sum(p_past, axis=1, keepdims=True) + jnp.sum(p_new, axis=1, keepdims=True)
            pv = (jnp.dot(p_past.astype(BF16), v_past, preferred_element_type=F32)
                  + jnp.dot(p_new.astype(BF16), v_new, preferred_element_type=F32))
            outs.append(pv / denom)
        o_ref[:, lanes] = jnp.where(lane < HEAD_DIM, outs[0], outs[1]).astype(o_ref.dtype)


def _attention_step(qaug, kn, vn, logf, cache_k, cache_v, cache_logf_t, t):
    n = qaug.shape[0]
    nseq = n // t
    past = cache_k.shape[1]
    row = lambda b: (b, 0)
    return pl.pallas_call(
        _attn_step_kernel,
        grid=(nseq,),
        in_specs=[
            pl.BlockSpec((t, AUG_WIDTH), row),
            pl.BlockSpec((t, WIDTH), row),
            pl.BlockSpec((t, WIDTH), row),
            pl.BlockSpec((t, 128), row),
            pl.BlockSpec((1, past, WIDTH), lambda b: (b, 0, 0)),
            pl.BlockSpec((1, past, WIDTH), lambda b: (b, 0, 0)),
            pl.BlockSpec((1, N_HEADS, past), lambda b: (b, 0, 0)),
        ],
        out_specs=pl.BlockSpec((t, WIDTH), row),
        out_shape=jax.ShapeDtypeStruct((n, WIDTH), BF16),
        compiler_params=pltpu.CompilerParams(
            dimension_semantics=("arbitrary",), vmem_limit_bytes=VMEM_LIMIT_BYTES),
        name="fox_attention_step",
    )(qaug, kn, vn, logf, cache_k, cache_v, cache_logf_t)


def _out_kernel(x_ref, of_ref, orw_ref, wof_ref, worw_ref, g_ref, wg_ref, wu_ref, wo_ref, o_ref):
    mix = lax.dot_general(of_ref[0], wof_ref[...], (((0,), (0,)), ((), ())), preferred_element_type=F32)
    mix = mix + jnp.dot(orw_ref[...], worw_ref[...], preferred_element_type=F32)
    o_ref[...] = _swiglu_half_step(x_ref[...] + mix, g_ref[...], wg_ref, wu_ref, wo_ref)


def _out_ffn(x, o_fox_t, o_rwkv, wo_fox, wo_rwkv, g, wg, wu, wo, tm):
    n = x.shape[0]
    per_fox_tile = o_fox_t.shape[2] // tm
    row = lambda i: (i, 0)
    return pl.pallas_call(
        _out_kernel,
        grid=(n // tm,),
        in_specs=[
            pl.BlockSpec((tm, D_MODEL), row),
            pl.BlockSpec((1, WIDTH, tm), lambda i: (i // per_fox_tile, 0, i % per_fox_tile)),
            pl.BlockSpec((tm, WIDTH), row),
            _const_spec((WIDTH, D_MODEL)),
            _const_spec((WIDTH, D_MODEL)),
            _const_spec((1, D_MODEL)),
            _const_spec((D_MODEL, D_FF)),
            _const_spec((D_MODEL, D_FF)),
            _const_spec((D_FF, D_MODEL)),
        ],
        out_specs=pl.BlockSpec((tm, D_MODEL), row),
        out_shape=jax.ShapeDtypeStruct((n, D_MODEL), F32),
        compiler_params=pltpu.CompilerParams(
            dimension_semantics=("arbitrary",), vmem_limit_bytes=VMEM_LIMIT_BYTES),
        name="out_proj_ffn",
    )(x, o_fox_t, o_rwkv, wo_fox, wo_rwkv, g, wg, wu, wo)


PROMPT_TILE = 512
ATTN_QUERY_TILE = 1024
RWKV_TILE = 256
RWKV_CHUNK = 64


def kernel(x_prompt, x_sample, cache_fox_k, cache_fox_v, cache_fox_logf, state_rwkv, state_rwkv_shift, norm_ffn1_g, ffn1_w_in, ffn1_w_out, norm_mix_g, w_in, w_out, fox_b_f, fox_q_norm_g, fox_k_norm_g, rwkv_mu, rwkv_w0, rwkv_w2, rwkv_a0, rwkv_a2, rwkv_g2, rwkv_k_k, rwkv_k_a, rwkv_r_k, rwkv_ln_g, rwkv_ln_b, norm_ffn2_g, ffn2_w_in, ffn2_w_out):
    depth = norm_ffn1_g.shape[0]
    bp, seq, _ = x_prompt.shape
    bs, dec_seq, _ = x_sample.shape
    assert bp == 1, "the prompt path assumes a single stream"
    n_p, n_s = bp * seq, bs * dec_seq
    yp = x_prompt.reshape(n_p, D_MODEL)
    ys = x_sample.reshape(n_s, D_MODEL)
    prompt_states = ([], [], [], [], [])
    sample_states = ([], [], [], [], [])
    for l in range(depth):
        ffn1 = (norm_ffn1_g[l][None], ffn1_w_in[l, :, :D_FF].astype(BF16), ffn1_w_in[l, :, D_FF:].astype(BF16),
                ffn1_w_out[l].astype(BF16))
        ffn2 = (norm_ffn2_g[l][None], ffn2_w_in[l, :, :D_FF].astype(BF16), ffn2_w_in[l, :, D_FF:].astype(BF16),
                ffn2_w_out[l].astype(BF16))
        proj_w = _prep_proj_weights(w_in[l], fox_b_f[l], fox_q_norm_g[l], fox_k_norm_g[l])
        rwkv_w = _prep_rwkv_weights(rwkv_mu[l], rwkv_w0[l], rwkv_w2[l], rwkv_a0[l], rwkv_a2[l], rwkv_g2[l],
                                    rwkv_k_k[l], rwkv_k_a[l], rwkv_r_k[l], rwkv_ln_g[l], rwkv_ln_b[l])
        wo_fox = w_out[l, :WIDTH].astype(BF16)
        wo_rwkv = w_out[l, WIDTH:].astype(BF16)
        mix_g = norm_mix_g[l][None]

        x1 = _ffn(yp, *ffn1, PROMPT_TILE)
        qaug, kn, kaug, v, vt, logf, prw, base = _proj(x1, mix_g, *proj_w, PROMPT_TILE)
        o_fox_t = _attention(base[:, 0, :N_HEADS].T, qaug, kaug, vt, ATTN_QUERY_TILE, PROMPT_TILE)
        o_rwkv, s_pairs = _rwkv(prw, jnp.zeros((bp, 1, RWKV_PROJ), F32),
                                jnp.zeros((bp, N_PAIRS, 128, 128), F32), rwkv_w, bp, RWKV_TILE, RWKV_CHUNK)
        yp = _out_ffn(x1, o_fox_t, o_rwkv, wo_fox, wo_rwkv, *ffn2, PROMPT_TILE)
        for acc, st in zip(prompt_states, (
                kn.reshape(bp, seq, N_HEADS, HEAD_DIM), v.reshape(bp, seq, N_HEADS, HEAD_DIM),
                logf[:, :N_HEADS].reshape(bp, seq, N_HEADS), _pairs_to_state(s_pairs),
                prw.reshape(bp, seq, RWKV_PROJ)[:, -1:])):
            acc.append(st)

        x1 = _ffn(ys, *ffn1, n_s)
        qaug, kn, kaug, v, vt, logf, prw, base = _proj(x1, mix_g, *proj_w, n_s)
        past = cache_fox_k.shape[2]
        o_fox = _attention_step(
            qaug, kn, v, logf, cache_fox_k[l].reshape(bs, past, WIDTH), cache_fox_v[l].reshape(bs, past, WIDTH),
            jnp.swapaxes(cache_fox_logf[l], 1, 2), dec_seq)
        o_rwkv, s_pairs = _rwkv(prw, state_rwkv_shift[l], _state_to_pairs(state_rwkv[l]), rwkv_w,
                                bs, dec_seq, dec_seq)
        ys = _out_ffn(x1, o_fox.T[None], o_rwkv, wo_fox, wo_rwkv, *ffn2, n_s)
        for acc, st in zip(sample_states, (
                kn.reshape(bs, dec_seq, N_HEADS, HEAD_DIM), v.reshape(bs, dec_seq, N_HEADS, HEAD_DIM),
                logf[:, :N_HEADS].reshape(bs, dec_seq, N_HEADS), _pairs_to_state(s_pairs),
                prw.reshape(bs, dec_seq, RWKV_PROJ)[:, -1:])):
            acc.append(st)

    return (yp.reshape(bp, seq, D_MODEL), ys.reshape(bs, dec_seq, D_MODEL),
            *(jnp.stack(a) for a in prompt_states), *(jnp.stack(a) for a in sample_states))
```

```python
import functools
import math

import jax
import jax.numpy as jnp
from jax import lax
from jax.experimental import pallas as pl
from jax.experimental.pallas import tpu as pltpu

F32 = jnp.float32
BF16 = jnp.bfloat16

D_MODEL = 1024
D_FF = 2816
HEAD_DIM = 64
N_HEADS = 8
WIDTH = N_HEADS * HEAD_DIM
RWKV_PROJ = 1792
RMS_EPS = 1e-6
GN_EPS = 64e-5
LOG2E = math.log2(math.e)

VMEM_LIMIT_BYTES = 56 * 1024 * 1024
FF_CHUNK = 512


def _const_spec(shape):
    return pl.BlockSpec(shape, lambda *_: (0,) * len(shape), pipeline_mode=pl.Buffered(1))


def _rms(x, g):
    return x * lax.rsqrt(jnp.mean(x * x, axis=-1, keepdims=True) + RMS_EPS) * g


def _swiglu_half_step(x, g, wg_ref, wu_ref, wo_ref):
    h = _rms(x, g).astype(BF16)
    acc = x
    for c0 in range(0, D_FF, FF_CHUNK):
        c1 = min(c0 + FF_CHUNK, D_FF)
        gate = jnp.dot(h, wg_ref[:, c0:c1], preferred_element_type=F32)
        up = jnp.dot(h, wu_ref[:, c0:c1], preferred_element_type=F32)
        a = (gate * jax.nn.sigmoid(gate) * up).astype(BF16)
        acc = acc + 0.5 * jnp.dot(a, wo_ref[c0:c1, :], preferred_element_type=F32)
    return acc


def _ffn_kernel(x_ref, g_ref, wg_ref, wu_ref, wo_ref, o_ref):
    o_ref[...] = _swiglu_half_step(x_ref[...], g_ref[...], wg_ref, wu_ref, wo_ref)


def _ffn(x, g, wg, wu, wo, tm):
    n = x.shape[0]
    return pl.pallas_call(
        _ffn_kernel,
        grid=(n // tm,),
        in_specs=[
            pl.BlockSpec((tm, D_MODEL), lambda i: (i, 0)),
            _const_spec((1, D_MODEL)),
            _const_spec((D_MODEL, D_FF)),
            _const_spec((D_MODEL, D_FF)),
            _const_spec((D_FF, D_MODEL)),
        ],
        out_specs=pl.BlockSpec((tm, D_MODEL), lambda i: (i, 0)),
        out_shape=jax.ShapeDtypeStruct((n, D_MODEL), F32),
        compiler_params=pltpu.CompilerParams(
            dimension_semantics=("arbitrary",), vmem_limit_bytes=VMEM_LIMIT_BYTES),
        name="ffn",
    )(x, g, wg, wu, wo)


PROJ_COLS = 3 * WIDTH + RWKV_PROJ + 128
V_ROWS = HEAD_DIM + 16
N_SPLIT = 3
PAIR_LANES = 256
AUG_WIDTH = (N_HEADS // 2) * PAIR_LANES


def _split_bf16(x, n):
    parts = []
    for _ in range(n):
        p = x.astype(BF16)
        parts.append(p)
        x = x - p.astype(F32)
    return parts


def _head_sums(x, bd, terms=2):
    return sum(jnp.dot(part, bd, preferred_element_type=F32) for part in _split_bf16(x, terms))


def _log_sigmoid(x):
    return jnp.minimum(x, 0.0) - jnp.log1p(jnp.exp(-jnp.abs(x)))


def _proj_kernel(x_ref, g_ref, w_ref, bf_ref, qg_ref, kg_ref, bd_ref, eq_ref, ek_ref, oneq_ref, onek_ref,
                 qaug_ref, k_ref, kaug_ref, v_ref, vt_ref, logf_ref, prw_ref, base_ref, carry_ref):
    i = pl.program_id(0)
    tm = x_ref.shape[0]

    @pl.when(i == 0)
    def _():
        carry_ref[...] = jnp.zeros_like(carry_ref)

    h = _rms(x_ref[...], g_ref[...]).astype(BF16)
    proj = jnp.dot(h, w_ref[...], preferred_element_type=F32)
    q = proj[:, :WIDTH]
    k = proj[:, WIDTH:2 * WIDTH]
    v = proj[:, 2 * WIDTH:3 * WIDTH]
    prw_ref[...] = proj[:, 3 * WIDTH:3 * WIDTH + RWKV_PROJ]
    f = proj[:, 3 * WIDTH + RWKV_PROJ:]

    bd = bd_ref[...]
    qn = q * lax.rsqrt(_head_sums(q * q, bd, 1) * (1.0 / HEAD_DIM) + RMS_EPS) * qg_ref[...]
    kn = k * lax.rsqrt(_head_sums(k * k, bd, 1) * (1.0 / HEAD_DIM) + RMS_EPS) * kg_ref[...]
    k_ref[...] = kn
    v_ref[...] = v
    v_t = v.T.astype(BF16)
    for hd in range(N_HEADS):
        vt_ref[0, hd * V_ROWS:hd * V_ROWS + HEAD_DIM, :] = v_t[hd * HEAD_DIM:(hd + 1) * HEAD_DIM, :]
        vt_ref[0, hd * V_ROWS + HEAD_DIM:(hd + 1) * V_ROWS, :] = jnp.ones((V_ROWS - HEAD_DIM, tm), BF16)

    lane = lax.broadcasted_iota(jnp.int32, f.shape, 1)
    logf = jnp.where(lane < N_HEADS, _log_sigmoid(f + bf_ref[...]), 0.0)
    logf_ref[...] = logf

    row = lax.broadcasted_iota(jnp.int32, (tm, tm), 0)
    col = lax.broadcasted_iota(jnp.int32, (tm, tm), 1)
    tri = jnp.where(col <= row, 1.0, 0.0).astype(BF16)
    cl = jnp.zeros_like(logf)
    for part in _split_bf16(logf, N_SPLIT):
        cl = cl + jnp.dot(tri, part, preferred_element_type=F32)
    c2 = cl * LOG2E
    base_ref[0] = jnp.broadcast_to(carry_ref[...], base_ref.shape[1:])
    carry_ref[...] = carry_ref[...] + c2[tm - 1:tm, :]

    parts = jnp.concatenate(_split_bf16(c2, N_SPLIT), axis=1)
    aug_q = (jnp.dot(parts, eq_ref[...], preferred_element_type=F32) + oneq_ref[...]).astype(BF16)
    aug_k = (jnp.dot(parts, ek_ref[...], preferred_element_type=F32) + onek_ref[...]).astype(BF16)
    qs = (qn * (HEAD_DIM ** -0.5 * LOG2E)).astype(BF16)
    ks = kn.astype(BF16)
    for p in range(N_HEADS // 2):
        qaug_ref[:, p * PAIR_LANES:p * PAIR_LANES + 128] = qs[:, p * 128:(p + 1) * 128]
        qaug_ref[:, p * PAIR_LANES + 128:(p + 1) * PAIR_LANES] = aug_q[:, p * 128:(p + 1) * 128]
        kaug_ref[:, p * PAIR_LANES:p * PAIR_LANES + 128] = ks[:, p * 128:(p + 1) * 128]
        kaug_ref[:, p * PAIR_LANES + 128:(p + 1) * PAIR_LANES] = aug_k[:, p * 128:(p + 1) * 128]


def _carrier_constants():
    eq = [[0.0] * WIDTH for _ in range(N_SPLIT * 128)]
    ek = [[0.0] * WIDTH for _ in range(N_SPLIT * 128)]
    oneq = [0.0] * WIDTH
    onek = [0.0] * WIDTH
    for hd in range(N_HEADS):
        off = (hd // 2) * 128 + (hd % 2) * 8
        for s in range(N_SPLIT):
            eq[s * 128 + hd][off + s] = 1.0
            ek[s * 128 + hd][off + N_SPLIT + s] = -1.0
            oneq[off + N_SPLIT + s] = 1.0
            onek[off + s] = 1.0
    return (jnp.array(eq, BF16), jnp.array(ek, BF16), jnp.array([oneq], F32), jnp.array([onek], F32))


def _prep_proj_weights(w_in, fox_b_f, q_norm_g, k_norm_g):
    fox_cols = 3 * WIDTH
    w_f = jnp.pad(w_in[:, fox_cols:fox_cols + N_HEADS], ((0, 0), (0, 128 - N_HEADS)))
    w_all = jnp.concatenate([w_in[:, :fox_cols], w_in[:, fox_cols + N_HEADS:], w_f], axis=1).astype(BF16)
    bf_pad = jnp.pad(fox_b_f, (0, 128 - N_HEADS))[None]
    return w_all, bf_pad, jnp.tile(q_norm_g, N_HEADS)[None], jnp.tile(k_norm_g, N_HEADS)[None]


def _proj(x, g, w_all, bf_pad, qg, kg, tm):
    n = x.shape[0]
    nblk = n // tm
    seg = jnp.arange(WIDTH) // HEAD_DIM
    bd = (seg[:, None] == seg[None, :]).astype(BF16)
    eq, ek, oneq, onek = _carrier_constants()
    row = lambda i: (i, 0)
    out_shape = (
        jax.ShapeDtypeStruct((n, AUG_WIDTH), BF16),
        jax.ShapeDtypeStruct((n, WIDTH), F32),
        jax.ShapeDtypeStruct((n, AUG_WIDTH), BF16),
        jax.ShapeDtypeStruct((n, WIDTH), F32),
        jax.ShapeDtypeStruct((nblk, N_HEADS * V_ROWS, tm), BF16),
        jax.ShapeDtypeStruct((n, 128), F32),
        jax.ShapeDtypeStruct((n, RWKV_PROJ), F32),
        jax.ShapeDtypeStruct((nblk, 8, 128), F32),
    )
    out_specs = (
        pl.BlockSpec((tm, AUG_WIDTH), row),
        pl.BlockSpec((tm, WIDTH), row),
        pl.BlockSpec((tm, AUG_WIDTH), row),
        pl.BlockSpec((tm, WIDTH), row),
        pl.BlockSpec((1, N_HEADS * V_ROWS, tm), lambda i: (i, 0, 0)),
        pl.BlockSpec((tm, 128), row),
        pl.BlockSpec((tm, RWKV_PROJ), row),
        pl.BlockSpec((1, 8, 128), lambda i: (i, 0, 0)),
    )
    return pl.pallas_call(
        _proj_kernel,
        grid=(nblk,),
        in_specs=[
            pl.BlockSpec((tm, D_MODEL), row),
            _const_spec((1, D_MODEL)),
            _const_spec((D_MODEL, PROJ_COLS)),
            _const_spec((1, 128)),
            _const_spec((1, WIDTH)),
            _const_spec((1, WIDTH)),
            _const_spec((WIDTH, WIDTH)),
            _const_spec((N_SPLIT * 128, WIDTH)),
            _const_spec((N_SPLIT * 128, WIDTH)),
            _const_spec((1, WIDTH)),
            _const_spec((1, WIDTH)),
        ],
        out_specs=out_specs,
        out_shape=out_shape,
        scratch_shapes=[pltpu.VMEM((1, 128), F32)],
        compiler_params=pltpu.CompilerParams(
            dimension_semantics=("arbitrary",), vmem_limit_bytes=VMEM_LIMIT_BYTES),
        name="mix_proj",
    )(x, g, w_all, bf_pad, qg, kg, bd, eq, ek, oneq, onek)


NEG_BIG = -1e30
STRIP = 256


def _attn_kernel(base_ref, q_ref, k_ref, vt_ref, o_ref,
                 qt_ref, s0_ref, s1_ref, x0_ref, x1_ref, p0_ref, p1_ref, m_ref, alpha_ref, acc_ref, *, tk):
    pair = pl.program_id(0)
    qi = pl.program_id(1)
    tq = q_ref.shape[0]
    sub = tq // tk
    n_steps = (qi + 1) * sub
    assert sub in (1, 2), "the first (peeled) trip must hold every diagonal block"
    strips = [(e, h) for e in range(2) for h in range(tq // STRIP)]

    def cols(e, h):
        return slice(e * tq + h * STRIP, e * tq + (h + 1) * STRIP)

    lane = lax.broadcasted_iota(jnp.int32, (tq, PAIR_LANES), 1)
    q = q_ref[...].astype(F32)
    for e in range(2):
        f0 = e * HEAD_DIM
        c0 = 128 + e * 8
        keep = ((lane >= f0) & (lane < f0 + HEAD_DIM)) | ((lane >= c0) & (lane < c0 + 2 * N_SPLIT))
        qt_ref[:, e * tq:(e + 1) * tq] = jnp.where(keep, q, 0.0).T.astype(BF16)

    def block_of(step):
        return jnp.where(step < sub, qi * sub + step, step - sub)

    def offset(step, e, h):
        hd = 2 * pair + e
        d = base_ref[hd, qi * sub + (h * STRIP) // tk] - base_ref[hd, block_of(step)]
        return jnp.where(step < n_steps, d, NEG_BIG)

    r = lax.broadcasted_iota(jnp.int32, (tk, STRIP), 0)
    c = lax.broadcasted_iota(jnp.int32, (tk, STRIP), 1)

    def scores(step, s_ref, smax_ref, e, h, diag=None):
        blk = block_of(jnp.minimum(step, n_steps))
        kc = k_ref[pl.ds(pl.multiple_of(blk * tk, tk), tk), :]
        s = jnp.dot(kc, qt_ref[:, cols(e, h)], preferred_element_type=F32)
        if diag is not None:
            s = jnp.where(r + diag * tk <= c + h * STRIP, s, NEG_BIG)
        s_ref[:, cols(e, h)] = s
        smax_ref[:, cols(e, h)] = jnp.max(s, axis=0, keepdims=True)

    def softmax(step, s_ref, smax_ref, p_ref):
        for e, h in strips:
            cs = cols(e, h)
            d = offset(step, e, h)
            m_old = m_ref[:, cs]
            m_new = jnp.maximum(m_old, smax_ref[:, cs] + d)
            m_ref[:, cs] = m_new
            alpha_ref[:, cs] = jnp.exp2(m_old - m_new)
            p_ref[:, cs] = jnp.exp2((s_ref[:, cs] - (m_new - d)).astype(BF16))

    def accumulate(step, p_ref, e, h):
        vt = vt_ref[block_of(step), e * V_ROWS:(e + 1) * V_ROWS, :]
        pv = jnp.dot(vt, p_ref[:, cols(e, h)], preferred_element_type=F32)
        hs = slice(h * STRIP, (h + 1) * STRIP)
        acc_ref[e, :, hs] = alpha_ref[:, cols(e, h)] * acc_ref[e, :, hs] + pv

    def two_steps(k, first=False):
        for e, h in strips:
            if not first:
                accumulate(k - 1, p1_ref, e, h)
            scores(k + 1, s1_ref, x1_ref, e, h, diag=1 if (first and sub > 1) else None)
        softmax(k, s0_ref, x0_ref, p0_ref)
        for e, h in strips:
            accumulate(k, p0_ref, e, h)
            scores(k + 2, s0_ref, x0_ref, e, h)
        softmax(k + 1, s1_ref, x1_ref, p1_ref)

    m_ref[...] = jnp.full(m_ref.shape, NEG_BIG, F32)
    acc_ref[...] = jnp.zeros_like(acc_ref)
    for e, h in strips:
        scores(0, s0_ref, x0_ref, e, h, diag=0)
    two_steps(0, first=True)

    def trip(it, carry):
        two_steps(2 * it)
        return carry

    n_trips = (n_steps + 1) // 2
    lax.fori_loop(1, n_trips, trip, 0)
    for e, h in strips:
        accumulate(2 * n_trips - 1, p1_ref, e, h)
    for e in range(2):
        acc = acc_ref[e]
        o_ref[0, e * HEAD_DIM:(e + 1) * HEAD_DIM, :] = (
            acc[:HEAD_DIM] / acc[HEAD_DIM:HEAD_DIM + 1]).astype(o_ref.dtype)


def _attention(base, qaug, kaug, vt, tq, tk):
    n = qaug.shape[0]
    grid_spec = pltpu.PrefetchScalarGridSpec(
        num_scalar_prefetch=1,
        grid=(N_HEADS // 2, n // tq),
        in_specs=[
            pl.BlockSpec((tq, PAIR_LANES), lambda p, i, b: (i, p)),
            pl.BlockSpec((n, PAIR_LANES), lambda p, i, b: (0, p)),
            pl.BlockSpec((n // tk, 2 * V_ROWS, tk), lambda p, i, b: (0, p, 0)),
        ],
        out_specs=pl.BlockSpec((1, 2 * HEAD_DIM, tq), lambda p, i, b: (i, p, 0)),
        scratch_shapes=[
            pltpu.VMEM((PAIR_LANES, 2 * tq), BF16),
            pltpu.VMEM((tk, 2 * tq), F32),
            pltpu.VMEM((tk, 2 * tq), F32),
            pltpu.VMEM((1, 2 * tq), F32),
            pltpu.VMEM((1, 2 * tq), F32),
            pltpu.VMEM((tk, 2 * tq), BF16),
            pltpu.VMEM((tk, 2 * tq), BF16),
            pltpu.VMEM((1, 2 * tq), F32),
            pltpu.VMEM((1, 2 * tq), F32),
            pltpu.VMEM((2, V_ROWS, tq), F32),
        ],
    )
    return pl.pallas_call(
        functools.partial(_attn_kernel, tk=tk),
        grid_spec=grid_spec,
        out_shape=jax.ShapeDtypeStruct((n // tq, WIDTH, tq), BF16),
        compiler_params=pltpu.CompilerParams(
            dimension_semantics=("arbitrary", "arbitrary"), vmem_limit_bytes=VMEM_LIMIT_BYTES),
        name="fox_attention",
    )(base, qaug, kaug, vt)


N_PAIRS = N_HEADS // 2


def _mm(a, b):
    return jnp.dot(a.astype(BF16), b.astype(BF16), preferred_element_type=F32)


def _mm_nt(a, b):
    return lax.dot_general(a.astype(BF16), b.astype(BF16), (((1,), (1,)), ((), ())), preferred_element_type=F32)


def _mm_tn(a, b):
    return lax.dot_general(a.astype(BF16), b.astype(BF16), (((0,), (0,)), ((), ())), preferred_element_type=F32)


def _rwkv_kernel(prw_ref, shift0_ref, s0_ref, mu_ref, w0_ref, a0_ref, w2a2_ref, g2_ref, kk_ref, ka_ref, rk_ref,
                 lng_ref, lnb_ref, bd_ref,
                 o_ref, sout_ref,
                 prev_ref, h_ref, y_ref,
                 *, chunk):
    t = pl.program_id(1)
    nt = pl.num_programs(1)
    tr = prw_ref.shape[0]
    n_chunks = tr // chunk

    @pl.when(t == 0)
    def _():
        prev_ref[...] = shift0_ref[0]
        h_ref[...] = s0_ref[0]

    prw = prw_ref[...]
    rolled = pltpu.roll(prw, 1, axis=0)
    row = lax.broadcasted_iota(jnp.int32, prw.shape, 0)
    prev = jnp.where(row == 0, prev_ref[...], rolled)
    prev_ref[...] = prw[tr - 1:tr, :]
    xs = prw + (prev - prw) * mu_ref[...]

    r = xs[:, :WIDTH]
    kr = xs[:, WIDTH:2 * WIDTH]
    vr = xs[:, 2 * WIDTH:3 * WIDTH]
    wa = xs[:, 3 * WIDTH:3 * WIDTH + 128]
    gd = xs[:, 3 * WIDTH + 128:]
    lane128 = lax.broadcasted_iota(jnp.int32, wa.shape, 1)
    wa_in = jnp.where(lane128 < 64, jnp.tanh(wa), wa).astype(BF16)
    lora = jnp.dot(wa_in, w2a2_ref[...], preferred_element_type=F32)
    w_log = _log_sigmoid(w0_ref[...] + lora[:, :WIDTH]) - 0.5
    lw = -jnp.exp(w_log)
    lr = jax.nn.sigmoid(a0_ref[...] + lora[:, WIDTH:])
    gate = jnp.dot(jax.nn.sigmoid(gd).astype(BF16), g2_ref[...], preferred_element_type=F32)

    bd = bd_ref[...]
    kk = kr * kk_ref[...]
    kk = kk / jnp.maximum(jnp.sqrt(_head_sums(kk * kk, bd)), 1e-12)
    k = kr * (1.0 + (lr - 1.0) * ka_ref[...])
    b = kk * lr

    ri = lax.broadcasted_iota(jnp.int32, (tr, tr), 0)
    ci = lax.broadcasted_iota(jnp.int32, (tr, tr), 1)
    tri = jnp.where((ci <= ri) & (ci // chunk == ri // chunk), 1.0, 0.0).astype(BF16)
    cs = jnp.zeros_like(lw)
    for part in _split_bf16(lw, N_SPLIT):
        cs = cs + jnp.dot(tri, part, preferred_element_type=F32)
    e_pos = jnp.exp(cs)
    e_neg = jnp.exp(-cs)
    at = (-kk * jnp.exp(cs - lw)).astype(BF16)
    rt = (r * e_pos).astype(BF16)
    bt = (b * e_neg).astype(BF16)
    kt = (k * e_neg).astype(BF16)
    vb = vr.astype(BF16)
    g_last, bh_rows, kh_rows = [], [], []
    for c in range(n_chunks):
        rows = slice(c * chunk, (c + 1) * chunk)
        last = cs[(c + 1) * chunk - 1:(c + 1) * chunk, :]
        to_end = jnp.exp(last - cs[rows, :])
        bh_rows.append((b[rows, :] * to_end).astype(BF16))
        kh_rows.append((k[rows, :] * to_end).astype(BF16))
        g_last.append(jnp.exp(last))

    c2 = 2 * chunk
    lane = lax.broadcasted_iota(jnp.int32, (chunk, 128), 1)
    even = lane < HEAD_DIM

    def stack(x, c, p):
        x = x[c * chunk:(c + 1) * chunk, p * 128:(p + 1) * 128]
        zero = jnp.zeros_like(x)
        return jnp.concatenate([jnp.where(even, x, zero), jnp.where(even, zero, x)], axis=0)

    rr = lax.broadcasted_iota(jnp.int32, (c2, c2), 0)
    cc = lax.broadcasted_iota(jnp.int32, (c2, c2), 1)
    same = (rr // chunk) == (cc // chunk)
    strict = same & (cc < rr)
    incl = same & (cc <= rr)
    eye = jnp.where(rr == cc, 1.0, 0.0)
    r128 = lax.broadcasted_iota(jnp.int32, (128, 128), 0)
    c128 = lax.broadcasted_iota(jnp.int32, (128, 128), 1)
    diag128 = r128 == c128

    insts = [(c, p) for c in range(n_chunks) for p in range(N_PAIRS)]
    a_s = [stack(at, c, p) for c, p in insts]
    r_s = [stack(rt, c, p) for c, p in insts]
    b_s = [stack(bt, c, p) for c, p in insts]
    k_s = [stack(kt, c, p) for c, p in insts]
    v_s = [stack(vb, c, p) for c, p in insts]
    bh_s = [stack(bh_rows[c], 0, p) for c, p in insts]
    kh_s = [stack(kh_rows[c], 0, p) for c, p in insts]
    wide = c2 % 128 == 0

    def pair(mm, lhs, r1, r2, axis=1):
        if not wide:
            return mm(lhs, r1), mm(lhs, r2)
        out = mm(lhs, jnp.concatenate([r1.astype(BF16), r2.astype(BF16)], axis=axis))
        return out[:, :out.shape[1] // 2], out[:, out.shape[1] // 2:]

    ar_s = [jnp.concatenate([a, rr_], axis=0) for a, rr_ in zip(a_s, r_s)]
    gbk = [pair(_mm_nt, x, y, z, axis=0) for x, y, z in zip(ar_s, b_s, k_s)]
    a_ab = [jnp.where(strict, gb[:c2], 0.0) for gb, _ in gbk]
    a_ak = [jnp.where(strict, gk[:c2], 0.0).astype(BF16) for _, gk in gbk]
    a_rb = [jnp.where(incl, gb[c2:], 0.0).astype(BF16) for gb, _ in gbk]
    a_rk = [jnp.where(incl, gk[c2:], 0.0).astype(BF16) for _, gk in gbk]
    x1 = [_mm(m, v) for m, v in zip(a_ak, v_s)]
    levels = int(math.log2(chunk))
    pw = [x.astype(BF16) for x in a_ab]
    inv = [eye + x for x in a_ab]
    pw = [_mm(x, x).astype(BF16) for x in pw]
    for j in range(1, levels):
        if j < levels - 1:
            sq_inc = [pair(_mm, x, x, i) for x, i in zip(pw, inv)]
            pw = [sq.astype(BF16) for sq, _ in sq_inc]
            inv = [i + inc for i, (_, inc) in zip(inv, sq_inc)]
        else:
            inv = [i + _mm(x, i) for x, i in zip(pw, inv)]
    inv = [i.astype(BF16) for i in inv]
    wm_u0 = [pair(_mm, i, a, x) for i, a, x in zip(inv, a_s, x1)]
    wm = [w.astype(BF16) for w, _ in wm_u0]
    u0 = [u.astype(BF16) for _, u in wm_u0]
    bh_wu = [pair(_mm_tn, bh, w, u) for bh, w, u in zip(bh_s, wm, u0)]
    rb_wu = [pair(_mm, m, w, u) for m, w, u in zip(a_rb, wm, u0)]
    mb = [x.astype(BF16) for x, _ in bh_wu]
    rm = [(rr_.astype(F32) + x).astype(BF16) for rr_, (x, _) in zip(r_s, rb_wu)]
    y0 = [x + _mm(m2, v) for (_, x), m2, v in zip(rb_wu, a_rk, v_s)]
    n0 = [x + _mm_tn(kh, v) for (_, x), kh, v in zip(bh_wu, kh_s, v_s)]

    states = [h_ref[p] for p in range(N_PAIRS)]
    for i, (c, p) in enumerate(insts):
        h = states[p]
        hb = h.astype(BF16)
        y = _mm(rm[i], hb) + y0[i]
        y_ref[c * chunk:(c + 1) * chunk, p * 128:(p + 1) * 128] = y[:chunk] + y[chunk:]
        g_col = jnp.sum(jnp.where(diag128, g_last[c][:, p * 128:(p + 1) * 128], 0.0), axis=1, keepdims=True)
        states[p] = g_col * h + _mm(mb[i], hb) + n0[i]
    for p in range(N_PAIRS):
        h_ref[p] = states[p]

    y = y_ref[...]
    mean = _head_sums(y, bd, 1) * (1.0 / HEAD_DIM)
    yc = y - mean
    var = _head_sums(yc * yc, bd, 1) * (1.0 / HEAD_DIM)
    yn = yc * lax.rsqrt(var + GN_EPS) * lng_ref[...] + lnb_ref[...]
    bonus = _head_sums(r * k * rk_ref[...], bd, 1) * vr
    o_ref[...] = ((yn + bonus) * gate).astype(o_ref.dtype)

    @pl.when(t == nt - 1)
    def _():
        sout_ref[0] = h_ref[...]


def _rwkv(prw, shift0, s0bd, wts, nseq, tr, chunk):
    n = prw.shape[0]
    tiles = n // (nseq * tr)
    row = lambda s, t: (s * tiles + t, 0)
    vec = lambda w: _const_spec((1, w))
    scratch = [pltpu.VMEM((1, RWKV_PROJ), F32), pltpu.VMEM((N_PAIRS, 128, 128), F32),
               pltpu.VMEM((tr, WIDTH), F32)]
    return pl.pallas_call(
        functools.partial(_rwkv_kernel, chunk=chunk),
        grid=(nseq, tiles),
        in_specs=[
            pl.BlockSpec((tr, RWKV_PROJ), row),
            pl.BlockSpec((1, 1, RWKV_PROJ), lambda s, t: (s, 0, 0)),
            pl.BlockSpec((1, N_PAIRS, 128, 128), lambda s, t: (s, 0, 0, 0)),
            vec(RWKV_PROJ), vec(WIDTH), vec(WIDTH),
            _const_spec((128, 2 * WIDTH)), _const_spec((128, WIDTH)),
            vec(WIDTH), vec(WIDTH), vec(WIDTH), vec(WIDTH), vec(WIDTH),
            _const_spec((WIDTH, WIDTH)),
        ],
        out_specs=(
            pl.BlockSpec((tr, WIDTH), row),
            pl.BlockSpec((1, N_PAIRS, 128, 128), lambda s, t: (s, 0, 0, 0)),
        ),
        out_shape=(
            jax.ShapeDtypeStruct((n, WIDTH), BF16),
            jax.ShapeDtypeStruct((nseq, N_PAIRS, 128, 128), F32),
        ),
        scratch_shapes=scratch,
        compiler_params=pltpu.CompilerParams(
            dimension_semantics=("arbitrary", "arbitrary"), vmem_limit_bytes=VMEM_LIMIT_BYTES),
        name="rwkv7_mix",
    )(prw, shift0, s0bd, *wts)


def _prep_rwkv_weights(mu, w0, w2, a0, a2, g2, k_k, k_a, r_k, ln_g, ln_b):
    z = jnp.zeros((64, WIDTH), F32)
    w2a2 = jnp.concatenate([jnp.concatenate([w2, z], axis=1), jnp.concatenate([z, a2], axis=1)], axis=0)
    seg = jnp.arange(WIDTH) // HEAD_DIM
    bd = (seg[:, None] == seg[None, :]).astype(BF16)
    return (mu[None], w0[None], a0[None], w2a2.astype(BF16), g2.astype(BF16), k_k[None], k_a[None],
            r_k.reshape(1, WIDTH), ln_g[None], ln_b[None], bd)


def _state_to_pairs(s):
    n = s.shape[0]
    ht = jnp.swapaxes(s, -1, -2).reshape(n, N_PAIRS, 2, HEAD_DIM, HEAD_DIM)
    z = jnp.zeros_like(ht[:, :, 0])
    top = jnp.concatenate([ht[:, :, 0], z], axis=-1)
    bot = jnp.concatenate([z, ht[:, :, 1]], axis=-1)
    return jnp.concatenate([top, bot], axis=-2)


def _pairs_to_state(hbd):
    n = hbd.shape[0]
    even = hbd[:, :, :HEAD_DIM, :HEAD_DIM]
    odd = hbd[:, :, HEAD_DIM:, HEAD_DIM:]
    ht = jnp.stack([even, odd], axis=2).reshape(n, N_HEADS, HEAD_DIM, HEAD_DIM)
    return jnp.swapaxes(ht, -1, -2)


def _attn_step_kernel(q_ref, kn_ref, vn_ref, lf_ref, ck_ref, cv_ref, clf_ref, o_ref, after_ref):
    t = q_ref.shape[0]
    past = ck_ref.shape[1]

    @pl.when(pl.program_id(0) == 0)
    def _():
        mi = lax.broadcasted_iota(jnp.int32, (past, past), 0)
        ji = lax.broadcasted_iota(jnp.int32, (past, past), 1)
        after_ref[...] = jnp.where(mi > ji, 1.0, 0.0).astype(BF16)

    after = after_ref[...]
    suffix = jnp.zeros(clf_ref.shape[1:], F32)
    for part in _split_bf16(clf_ref[0], N_SPLIT):
        suffix = suffix + jnp.dot(part, after, preferred_element_type=F32)
    suffix = suffix * LOG2E
    ri = lax.broadcasted_iota(jnp.int32, (t, t), 0)
    ci = lax.broadcasted_iota(jnp.int32, (t, t), 1)
    causal = ci <= ri
    tri = jnp.where(causal, 1.0, 0.0).astype(BF16)
    cn = jnp.zeros(lf_ref.shape, F32)
    for part in _split_bf16(lf_ref[...], N_SPLIT):
        cn = cn + jnp.dot(tri, part, preferred_element_type=F32)
    cn = cn * LOG2E
    cn_t = cn.T

    lane = lax.broadcasted_iota(jnp.int32, (t, 128), 1)
    for p in range(N_PAIRS):
        q_pair = q_ref[:, p * PAIR_LANES:p * PAIR_LANES + 128]
        lanes = slice(p * 128, (p + 1) * 128)
        k_past = ck_ref[0, :, lanes].astype(BF16)
        v_past = cv_ref[0, :, lanes].astype(BF16)
        k_new = kn_ref[:, lanes].astype(BF16)
        v_new = vn_ref[:, lanes].astype(BF16)
        outs = []
        for e in range(2):
            hd = 2 * p + e
            mine = (lane >= e * HEAD_DIM) & (lane < (e + 1) * HEAD_DIM)
            qm = jnp.where(mine, q_pair, jnp.zeros_like(q_pair))
            cq = cn[:, hd:hd + 1]
            s_past = lax.dot_general(qm, k_past, (((1,), (1,)), ((), ())), preferred_element_type=F32)
            s_past = s_past + cq + suffix[hd:hd + 1, :]
            s_new = lax.dot_general(qm, k_new, (((1,), (1,)), ((), ())), preferred_element_type=F32)
            s_new = jnp.where(causal, s_new + cq - cn_t[hd:hd + 1, :], NEG_BIG)
            m = jnp.maximum(jnp.max(s_past, axis=1, keepdims=True), jnp.max(s_new, axis=1, keepdims=True))
            p_past = jnp.exp2(s_past - m)
            p_new = jnp.exp2(s_new - m)
            denom = jnp.sum(p_past, axis=1, keepdims=True) + jnp.sum(p_new, axis=1, keepdims=True)
            pv = (jnp.dot(p_past.astype(BF16), v_past, preferred_element_type=F32)
                  + jnp.dot(p_new.astype(BF16), v_new, preferred_element_type=F32))
            outs.append(pv / denom)
        o_ref[:, lanes] = jnp.where(lane < HEAD_DIM, outs[0], outs[1]).astype(o_ref.dtype)


def _attention_step(qaug, kn, vn, logf, cache_k, cache_v, cache_logf_t, t):
    n = qaug.shape[0]
    nseq = n // t
    past = cache_k.shape[1]
    row = lambda b: (b, 0)
    return pl.pallas_call(
        _attn_step_kernel,
        grid=(nseq,),
        in_specs=[
            pl.BlockSpec((t, AUG_WIDTH), row),
            pl.BlockSpec((t, WIDTH), row),
            pl.BlockSpec((t, WIDTH), row),
            pl.BlockSpec((t, 128), row),
            pl.BlockSpec((1, past, WIDTH), lambda b: (b, 0, 0)),
            pl.BlockSpec((1, past, WIDTH), lambda b: (b, 0, 0)),
            pl.BlockSpec((1, N_HEADS, past), lambda b: (b, 0, 0)),
        ],
        out_specs=pl.BlockSpec((t, WIDTH), row),
        out_shape=jax.ShapeDtypeStruct((n, WIDTH), BF16),
        scratch_shapes=[pltpu.VMEM((past, past), BF16)],
        compiler_params=pltpu.CompilerParams(
            dimension_semantics=("arbitrary",), vmem_limit_bytes=VMEM_LIMIT_BYTES),
        name="fox_attention_step",
    )(qaug, kn, vn, logf, cache_k, cache_v, cache_logf_t)


def _out_kernel(x_ref, of_ref, orw_ref, wof_ref, worw_ref, g_ref, wg_ref, wu_ref, wo_ref, o_ref):
    mix = lax.dot_general(of_ref[0], wof_ref[...], (((0,), (0,)), ((), ())), preferred_element_type=F32)
    mix = mix + jnp.dot(orw_ref[...], worw_ref[...], preferred_element_type=F32)
    o_ref[...] = _swiglu_half_step(x_ref[...] + mix, g_ref[...], wg_ref, wu_ref, wo_ref)


def _out_ffn(x, o_fox_t, o_rwkv, wo_fox, wo_rwkv, g, wg, wu, wo, tm):
    n = x.shape[0]
    per_fox_tile = o_fox_t.shape[2] // tm
    row = lambda i: (i, 0)
    return pl.pallas_call(
        _out_kernel,
        grid=(n // tm,),
        in_specs=[
            pl.BlockSpec((tm, D_MODEL), row),
            pl.BlockSpec((1, WIDTH, tm), lambda i: (i // per_fox_tile, 0, i % per_fox_tile)),
            pl.BlockSpec((tm, WIDTH), row),
            _const_spec((WIDTH, D_MODEL)),
            _const_spec((WIDTH, D_MODEL)),
            _const_spec((1, D_MODEL)),
            _const_spec((D_MODEL, D_FF)),
            _const_spec((D_MODEL, D_FF)),
            _const_spec((D_FF, D_MODEL)),
        ],
        out_specs=pl.BlockSpec((tm, D_MODEL), row),
        out_shape=jax.ShapeDtypeStruct((n, D_MODEL), F32),
        compiler_params=pltpu.CompilerParams(
            dimension_semantics=("arbitrary",), vmem_limit_bytes=VMEM_LIMIT_BYTES),
        name="out_proj_ffn",
    )(x, o_fox_t, o_rwkv, wo_fox, wo_rwkv, g, wg, wu, wo)


PROMPT_TILE = 512
ATTN_QUERY_TILE = 1024
RWKV_TILE = 256
RWKV_CHUNK = 64


def kernel(x_prompt, x_sample, cache_fox_k, cache_fox_v, cache_fox_logf, state_rwkv, state_rwkv_shift, norm_ffn1_g, ffn1_w_in, ffn1_w_out, norm_mix_g, w_in, w_out, fox_b_f, fox_q_norm_g, fox_k_norm_g, rwkv_mu, rwkv_w0, rwkv_w2, rwkv_a0, rwkv_a2, rwkv_g2, rwkv_k_k, rwkv_k_a, rwkv_r_k, rwkv_ln_g, rwkv_ln_b, norm_ffn2_g, ffn2_w_in, ffn2_w_out):
    depth = norm_ffn1_g.shape[0]
    bp, seq, _ = x_prompt.shape
    bs, dec_seq, _ = x_sample.shape
    assert bp == 1, "the prompt path assumes a single stream"
    n_p, n_s = bp * seq, bs * dec_seq
    yp = x_prompt.reshape(n_p, D_MODEL)
    ys = x_sample.reshape(n_s, D_MODEL)
    prompt_states = ([], [], [], [], [])
    sample_states = ([], [], [], [], [])
    for l in range(depth):
        ffn1 = (norm_ffn1_g[l][None], ffn1_w_in[l, :, :D_FF].astype(BF16), ffn1_w_in[l, :, D_FF:].astype(BF16),
                ffn1_w_out[l].astype(BF16))
        ffn2 = (norm_ffn2_g[l][None], ffn2_w_in[l, :, :D_FF].astype(BF16), ffn2_w_in[l, :, D_FF:].astype(BF16),
                ffn2_w_out[l].astype(BF16))
        proj_w = _prep_proj_weights(w_in[l], fox_b_f[l], fox_q_norm_g[l], fox_k_norm_g[l])
        rwkv_w = _prep_rwkv_weights(rwkv_mu[l], rwkv_w0[l], rwkv_w2[l], rwkv_a0[l], rwkv_a2[l], rwkv_g2[l],
                                    rwkv_k_k[l], rwkv_k_a[l], rwkv_r_k[l], rwkv_ln_g[l], rwkv_ln_b[l])
        wo_fox = w_out[l, :WIDTH].astype(BF16)
        wo_rwkv = w_out[l, WIDTH:].astype(BF16)
        mix_g = norm_mix_g[l][None]

        x1 = _ffn(yp, *ffn1, PROMPT_TILE)
        qaug, kn, kaug, v, vt, logf, prw, base = _proj(x1, mix_g, *proj_w, PROMPT_TILE)
        o_fox_t = _attention(base[:, 0, :N_HEADS].T, qaug, kaug, vt, ATTN_QUERY_TILE, PROMPT_TILE)
        o_rwkv, s_pairs = _rwkv(prw, jnp.zeros((bp, 1, RWKV_PROJ), F32),
                                jnp.zeros((bp, N_PAIRS, 128, 128), F32), rwkv_w, bp, RWKV_TILE, RWKV_CHUNK)
        yp = _out_ffn(x1, o_fox_t, o_rwkv, wo_fox, wo_rwkv, *ffn2, PROMPT_TILE)
        for acc, st in zip(prompt_states, (
                kn.reshape(bp, seq, N_HEADS, HEAD_DIM), v.reshape(bp, seq, N_HEADS, HEAD_DIM),
                logf[:, :N_HEADS].reshape(bp, seq, N_HEADS), _pairs_to_state(s_pairs),
                prw.reshape(bp, seq, RWKV_PROJ)[:, -1:])):
            acc.append(st)

        x1 = _ffn(ys, *ffn1, n_s)
        qaug, kn, kaug, v, vt, logf, prw, base = _proj(x1, mix_g, *proj_w, n_s)
        past = cache_fox_k.shape[2]
        o_fox = _attention_step(
            qaug, kn, v, logf, cache_fox_k[l].reshape(bs, past, WIDTH), cache_fox_v[l].reshape(bs, past, WIDTH),
            jnp.swapaxes(cache_fox_logf[l], 1, 2), dec_seq)
        o_rwkv, s_pairs = _rwkv(prw, state_rwkv_shift[l], _state_to_pairs(state_rwkv[l]), rwkv_w,
                                bs, dec_seq, dec_seq)
        ys = _out_ffn(x1, o_fox.T[None], o_rwkv, wo_fox, wo_rwkv, *ffn2, n_s)
        for acc, st in zip(sample_states, (
                kn.reshape(bs, dec_seq, N_HEADS, HEAD_DIM), v.reshape(bs, dec_seq, N_HEADS, HEAD_DIM),
                logf[:, :N_HEADS].reshape(bs, dec_seq, N_HEADS), _pairs_to_state(s_pairs),
                prw.reshape(bs, dec_seq, RWKV_PROJ)[:, -1:])):
            acc.append(st)

    return (yp.reshape(bp, seq, D_MODEL), ys.reshape(bs, dec_seq, D_MODEL),
            *(jnp.stack(a) for a in prompt_states), *(jnp.stack(a) for a in sample_states))
```

```python
import functools
import math

import jax
import jax.numpy as jnp
from jax import lax
from jax.experimental import pallas as pl
from jax.experimental.pallas import tpu as pltpu

F32 = jnp.float32
BF16 = jnp.bfloat16

D_MODEL = 1024
D_FF = 2816
HEAD_DIM = 64
N_HEADS = 8
WIDTH = N_HEADS * HEAD_DIM
RWKV_PROJ = 1792
RMS_EPS = 1e-6
GN_EPS = 64e-5
LOG2E = math.log2(math.e)

VMEM_LIMIT_BYTES = 56 * 1024 * 1024
FF_CHUNK = 256


def _const_spec(shape):
    return pl.BlockSpec(shape, lambda *_: (0,) * len(shape), pipeline_mode=pl.Buffered(1))


def _rms(x, g):
    return x * lax.rsqrt(jnp.mean(x * x, axis=-1, keepdims=True) + RMS_EPS) * g


def _swiglu_half_step(x, g, wg_ref, wu_ref, wo_ref):
    h = _rms(x, g).astype(BF16)
    acc = x
    for c0 in range(0, D_FF, FF_CHUNK):
        c1 = min(c0 + FF_CHUNK, D_FF)
        gate = jnp.dot(h, wg_ref[:, c0:c1], preferred_element_type=F32)
        up = jnp.dot(h, wu_ref[:, c0:c1], preferred_element_type=F32)
        a = (gate * jax.nn.sigmoid(gate) * up).astype(BF16)
        acc = acc + 0.5 * jnp.dot(a, wo_ref[c0:c1, :], preferred_element_type=F32)
    return acc


def _ffn_kernel(x_ref, g_ref, wg_ref, wu_ref, wo_ref, o_ref):
    o_ref[...] = _swiglu_half_step(x_ref[...], g_ref[...], wg_ref, wu_ref, wo_ref)


def _ffn(x, g, wg, wu, wo, tm):
    n = x.shape[0]
    return pl.pallas_call(
        _ffn_kernel,
        grid=(n // tm,),
        in_specs=[
            pl.BlockSpec((tm, D_MODEL), lambda i: (i, 0)),
            _const_spec((1, D_MODEL)),
            _const_spec((D_MODEL, D_FF)),
            _const_spec((D_MODEL, D_FF)),
            _const_spec((D_FF, D_MODEL)),
        ],
        out_specs=pl.BlockSpec((tm, D_MODEL), lambda i: (i, 0)),
        out_shape=jax.ShapeDtypeStruct((n, D_MODEL), F32),
        compiler_params=pltpu.CompilerParams(
            dimension_semantics=("arbitrary",), vmem_limit_bytes=VMEM_LIMIT_BYTES),
        name="ffn",
    )(x, g, wg, wu, wo)


PROJ_COLS = 3 * WIDTH + RWKV_PROJ + 128
V_ROWS = HEAD_DIM + 16
N_SPLIT = 3
PAIR_LANES = 256
AUG_WIDTH = (N_HEADS // 2) * PAIR_LANES


def _split_bf16(x, n):
    parts = []
    for _ in range(n):
        p = x.astype(BF16)
        parts.append(p)
        x = x - p.astype(F32)
    return parts


def _head_sums(x, bd, terms=2):
    return sum(jnp.dot(part, bd, preferred_element_type=F32) for part in _split_bf16(x, terms))


def _log_sigmoid(x):
    return jnp.minimum(x, 0.0) - jnp.log1p(jnp.exp(-jnp.abs(x)))


def _proj_kernel(x_ref, g_ref, w_ref, bf_ref, qg_ref, kg_ref, bd_ref, eq_ref, ek_ref, oneq_ref, onek_ref,
                 qaug_ref, k_ref, kaug_ref, v_ref, vt_ref, logf_ref, prw_ref, base_ref, carry_ref):
    i = pl.program_id(0)
    tm = x_ref.shape[0]

    @pl.when(i == 0)
    def _():
        carry_ref[...] = jnp.zeros_like(carry_ref)

    h = _rms(x_ref[...], g_ref[...]).astype(BF16)
    proj = jnp.dot(h, w_ref[...], preferred_element_type=F32)
    q = proj[:, :WIDTH]
    k = proj[:, WIDTH:2 * WIDTH]
    v = proj[:, 2 * WIDTH:3 * WIDTH]
    prw_ref[...] = proj[:, 3 * WIDTH:3 * WIDTH + RWKV_PROJ]
    f = proj[:, 3 * WIDTH + RWKV_PROJ:]

    bd = bd_ref[...]
    qn = q * lax.rsqrt(_head_sums(q * q, bd, 1) * (1.0 / HEAD_DIM) + RMS_EPS) * qg_ref[...]
    kn = k * lax.rsqrt(_head_sums(k * k, bd, 1) * (1.0 / HEAD_DIM) + RMS_EPS) * kg_ref[...]
    k_ref[...] = kn
    v_ref[...] = v
    v_t = v.T.astype(BF16)
    for hd in range(N_HEADS):
        vt_ref[0, hd * V_ROWS:hd * V_ROWS + HEAD_DIM, :] = v_t[hd * HEAD_DIM:(hd + 1) * HEAD_DIM, :]
        vt_ref[0, hd * V_ROWS + HEAD_DIM:(hd + 1) * V_ROWS, :] = jnp.ones((V_ROWS - HEAD_DIM, tm), BF16)

    lane = lax.broadcasted_iota(jnp.int32, f.shape, 1)
    logf = jnp.where(lane < N_HEADS, _log_sigmoid(f + bf_ref[...]), 0.0)
    logf_ref[...] = logf

    row = lax.broadcasted_iota(jnp.int32, (tm, tm), 0)
    col = lax.broadcasted_iota(jnp.int32, (tm, tm), 1)
    tri = jnp.where(col <= row, 1.0, 0.0).astype(BF16)
    cl = jnp.zeros_like(logf)
    for part in _split_bf16(logf, N_SPLIT):
        cl = cl + jnp.dot(tri, part, preferred_element_type=F32)
    c2 = cl * LOG2E
    base_ref[0] = jnp.broadcast_to(carry_ref[...], base_ref.shape[1:])
    carry_ref[...] = carry_ref[...] + c2[tm - 1:tm, :]

    parts = jnp.concatenate(_split_bf16(c2, N_SPLIT), axis=1)
    aug_q = (jnp.dot(parts, eq_ref[...], preferred_element_type=F32) + oneq_ref[...]).astype(BF16)
    aug_k = (jnp.dot(parts, ek_ref[...], preferred_element_type=F32) + onek_ref[...]).astype(BF16)
    qs = (qn * (HEAD_DIM ** -0.5 * LOG2E)).astype(BF16)
    ks = kn.astype(BF16)
    for p in range(N_HEADS // 2):
        qaug_ref[:, p * PAIR_LANES:p * PAIR_LANES + 128] = qs[:, p * 128:(p + 1) * 128]
        qaug_ref[:, p * PAIR_LANES + 128:(p + 1) * PAIR_LANES] = aug_q[:, p * 128:(p + 1) * 128]
        kaug_ref[:, p * PAIR_LANES:p * PAIR_LANES + 128] = ks[:, p * 128:(p + 1) * 128]
        kaug_ref[:, p * PAIR_LANES + 128:(p + 1) * PAIR_LANES] = aug_k[:, p * 128:(p + 1) * 128]


def _carrier_constants():
    eq = [[0.0] * WIDTH for _ in range(N_SPLIT * 128)]
    ek = [[0.0] * WIDTH for _ in range(N_SPLIT * 128)]
    oneq = [0.0] * WIDTH
    onek = [0.0] * WIDTH
    for hd in range(N_HEADS):
        off = (hd // 2) * 128 + (hd % 2) * 8
        for s in range(N_SPLIT):
            eq[s * 128 + hd][off + s] = 1.0
            ek[s * 128 + hd][off + N_SPLIT + s] = -1.0
            oneq[off + N_SPLIT + s] = 1.0
            onek[off + s] = 1.0
    return (jnp.array(eq, BF16), jnp.array(ek, BF16), jnp.array([oneq], F32), jnp.array([onek], F32))


def _prep_proj_weights(w_in, fox_b_f, q_norm_g, k_norm_g):
    fox_cols = 3 * WIDTH
    w_f = jnp.pad(w_in[:, fox_cols:fox_cols + N_HEADS], ((0, 0), (0, 128 - N_HEADS)))
    w_all = jnp.concatenate([w_in[:, :fox_cols], w_in[:, fox_cols + N_HEADS:], w_f], axis=1).astype(BF16)
    bf_pad = jnp.pad(fox_b_f, (0, 128 - N_HEADS))[None]
    return w_all, bf_pad, jnp.tile(q_norm_g, N_HEADS)[None], jnp.tile(k_norm_g, N_HEADS)[None]


def _proj(x, g, w_all, bf_pad, qg, kg, tm):
    n = x.shape[0]
    nblk = n // tm
    seg = jnp.arange(WIDTH) // HEAD_DIM
    bd = (seg[:, None] == seg[None, :]).astype(BF16)
    eq, ek, oneq, onek = _carrier_constants()
    row = lambda i: (i, 0)
    out_shape = (
        jax.ShapeDtypeStruct((n, AUG_WIDTH), BF16),
        jax.ShapeDtypeStruct((n, WIDTH), F32),
        jax.ShapeDtypeStruct((n, AUG_WIDTH), BF16),
        jax.ShapeDtypeStruct((n, WIDTH), F32),
        jax.ShapeDtypeStruct((nblk, N_HEADS * V_ROWS, tm), BF16),
        jax.ShapeDtypeStruct((n, 128), F32),
        jax.ShapeDtypeStruct((n, RWKV_PROJ), F32),
        jax.ShapeDtypeStruct((nblk, 8, 128), F32),
    )
    out_specs = (
        pl.BlockSpec((tm, AUG_WIDTH), row),
        pl.BlockSpec((tm, WIDTH), row),
        pl.BlockSpec((tm, AUG_WIDTH), row),
        pl.BlockSpec((tm, WIDTH), row),
        pl.BlockSpec((1, N_HEADS * V_ROWS, tm), lambda i: (i, 0, 0)),
        pl.BlockSpec((tm, 128), row),
        pl.BlockSpec((tm, RWKV_PROJ), row),
        pl.BlockSpec((1, 8, 128), lambda i: (i, 0, 0)),
    )
    return pl.pallas_call(
        _proj_kernel,
        grid=(nblk,),
        in_specs=[
            pl.BlockSpec((tm, D_MODEL), row),
            _const_spec((1, D_MODEL)),
            _const_spec((D_MODEL, PROJ_COLS)),
            _const_spec((1, 128)),
            _const_spec((1, WIDTH)),
            _const_spec((1, WIDTH)),
            _const_spec((WIDTH, WIDTH)),
            _const_spec((N_SPLIT * 128, WIDTH)),
            _const_spec((N_SPLIT * 128, WIDTH)),
            _const_spec((1, WIDTH)),
            _const_spec((1, WIDTH)),
        ],
        out_specs=out_specs,
        out_shape=out_shape,
        scratch_shapes=[pltpu.VMEM((1, 128), F32)],
        compiler_params=pltpu.CompilerParams(
            dimension_semantics=("arbitrary",), vmem_limit_bytes=VMEM_LIMIT_BYTES),
        name="mix_proj",
    )(x, g, w_all, bf_pad, qg, kg, bd, eq, ek, oneq, onek)


NEG_BIG = -1e30
STRIP = 256


def _attn_kernel(base_ref, q_ref, k_ref, vt_ref, o_ref,
                 qt_ref, s0_ref, s1_ref, x0_ref, x1_ref, p0_ref, p1_ref, m_ref, alpha_ref, acc_ref, *, tk):
    pair = pl.program_id(0)
    qi = pl.program_id(1)
    tq = q_ref.shape[0]
    sub = tq // tk
    n_steps = (qi + 1) * sub
    assert sub in (1, 2), "the first (peeled) trip must hold every diagonal block"
    strips = [(e, h) for e in range(2) for h in range(tq // STRIP)]

    def cols(e, h):
        return slice(e * tq + h * STRIP, e * tq + (h + 1) * STRIP)

    def strip_id(e, h):
        return e * (tq // STRIP) + h

    lane = lax.broadcasted_iota(jnp.int32, (tq, PAIR_LANES), 1)
    q = q_ref[...].astype(F32)
    for e in range(2):
        f0 = e * HEAD_DIM
        c0 = 128 + e * 8
        keep = ((lane >= f0) & (lane < f0 + HEAD_DIM)) | ((lane >= c0) & (lane < c0 + 2 * N_SPLIT))
        qt_ref[:, e * tq:(e + 1) * tq] = jnp.where(keep, q, 0.0).T.astype(BF16)

    def block_of(step):
        return jnp.where(step < sub, qi * sub + step, step - sub)

    def offset(step, e, h):
        hd = 2 * pair + e
        d = base_ref[hd, qi * sub + (h * STRIP) // tk] - base_ref[hd, block_of(step)]
        return jnp.where(step < n_steps, d, NEG_BIG)

    r = lax.broadcasted_iota(jnp.int32, (tk, STRIP), 0)
    c = lax.broadcasted_iota(jnp.int32, (tk, STRIP), 1)

    def scores(step, s_ref, smax_ref, e, h, diag=None):
        blk = block_of(jnp.minimum(step, n_steps))
        kc = k_ref[pl.ds(pl.multiple_of(blk * tk, tk), tk), :]
        s = jnp.dot(kc, qt_ref[:, cols(e, h)], preferred_element_type=F32)
        if diag is not None:
            s = jnp.where(r + diag * tk <= c + h * STRIP, s, NEG_BIG)
        s_ref[strip_id(e, h)] = s
        smax_ref[:, cols(e, h)] = jnp.max(s, axis=0, keepdims=True)

    def softmax(step, s_ref, smax_ref, p_ref):
        for e, h in strips:
            cs = cols(e, h)
            d = offset(step, e, h)
            m_old = m_ref[:, cs]
            m_new = jnp.maximum(m_old, smax_ref[:, cs] + d)
            m_ref[:, cs] = m_new
            alpha_ref[:, cs] = jnp.exp2(m_old - m_new)
            p_ref[strip_id(e, h)] = jnp.exp2((s_ref[strip_id(e, h)] - (m_new - d)).astype(BF16))

    def accumulate(step, p_ref, e, h):
        vt = vt_ref[block_of(step), e * V_ROWS:(e + 1) * V_ROWS, :]
        pv = jnp.dot(vt, p_ref[strip_id(e, h)], preferred_element_type=F32)
        hs = slice(h * STRIP, (h + 1) * STRIP)
        acc_ref[e, :, hs] = alpha_ref[:, cols(e, h)] * acc_ref[e, :, hs] + pv

    def two_steps(k, first=False):
        for e, h in strips:
            if not first:
                accumulate(k - 1, p1_ref, e, h)
            scores(k + 1, s1_ref, x1_ref, e, h, diag=1 if (first and sub > 1) else None)
        softmax(k, s0_ref, x0_ref, p0_ref)
        for e, h in strips:
            accumulate(k, p0_ref, e, h)
            scores(k + 2, s0_ref, x0_ref, e, h)
        softmax(k + 1, s1_ref, x1_ref, p1_ref)

    m_ref[...] = jnp.full(m_ref.shape, NEG_BIG, F32)
    acc_ref[...] = jnp.zeros_like(acc_ref)
    for e, h in strips:
        scores(0, s0_ref, x0_ref, e, h, diag=0)
    two_steps(0, first=True)

    def trip(it, carry):
        two_steps(2 * it)
        return carry

    n_trips = (n_steps + 1) // 2
    lax.fori_loop(1, n_trips, trip, 0)
    for e, h in strips:
        accumulate(2 * n_trips - 1, p1_ref, e, h)
    for e in range(2):
        acc = acc_ref[e]
        o_ref[0, e * HEAD_DIM:(e + 1) * HEAD_DIM, :] = (
            acc[:HEAD_DIM] / acc[HEAD_DIM:HEAD_DIM + 1]).astype(o_ref.dtype)


def _attention(base, qaug, kaug, vt, tq, tk):
    n = qaug.shape[0]
    grid_spec = pltpu.PrefetchScalarGridSpec(
        num_scalar_prefetch=1,
        grid=(N_HEADS // 2, n // tq),
        in_specs=[
            pl.BlockSpec((tq, PAIR_LANES), lambda p, i, b: (i, p)),
            pl.BlockSpec((n, PAIR_LANES), lambda p, i, b: (0, p)),
            pl.BlockSpec((n // tk, 2 * V_ROWS, tk), lambda p, i, b: (0, p, 0)),
        ],
        out_specs=pl.BlockSpec((1, 2 * HEAD_DIM, tq), lambda p, i, b: (i, p, 0)),
        scratch_shapes=[
            pltpu.VMEM((PAIR_LANES, 2 * tq), BF16),
            pltpu.VMEM((2 * tq // STRIP, tk, STRIP), F32),
            pltpu.VMEM((2 * tq // STRIP, tk, STRIP), F32),
            pltpu.VMEM((1, 2 * tq), F32),
            pltpu.VMEM((1, 2 * tq), F32),
            pltpu.VMEM((2 * tq // STRIP, tk, STRIP), BF16),
            pltpu.VMEM((2 * tq // STRIP, tk, STRIP), BF16),
            pltpu.VMEM((1, 2 * tq), F32),
            pltpu.VMEM((1, 2 * tq), F32),
            pltpu.VMEM((2, V_ROWS, tq), F32),
        ],
    )
    return pl.pallas_call(
        functools.partial(_attn_kernel, tk=tk),
        grid_spec=grid_spec,
        out_shape=jax.ShapeDtypeStruct((n // tq, WIDTH, tq), BF16),
        compiler_params=pltpu.CompilerParams(
            dimension_semantics=("arbitrary", "arbitrary"), vmem_limit_bytes=VMEM_LIMIT_BYTES),
        name="fox_attention",
    )(base, qaug, kaug, vt)


N_PAIRS = N_HEADS // 2


def _mm(a, b):
    return jnp.dot(a.astype(BF16), b.astype(BF16), preferred_element_type=F32)


def _mm_nt(a, b):
    return lax.dot_general(a.astype(BF16), b.astype(BF16), (((1,), (1,)), ((), ())), preferred_element_type=F32)


def _mm_tn(a, b):
    return lax.dot_general(a.astype(BF16), b.astype(BF16), (((0,), (0,)), ((), ())), preferred_element_type=F32)


def _rwkv_kernel(prw_ref, shift0_ref, s0_ref, mu_ref, w0_ref, a0_ref, w2a2_ref, g2_ref, kk_ref, ka_ref, rk_ref,
                 lng_ref, lnb_ref, bd_ref,
                 o_ref, sout_ref,
                 prev_ref, h_ref, y_ref,
                 *, chunk):
    t = pl.program_id(1)
    nt = pl.num_programs(1)
    tr = prw_ref.shape[0]
    n_chunks = tr // chunk

    @pl.when(t == 0)
    def _():
        prev_ref[...] = shift0_ref[0]
        h_ref[...] = s0_ref[0]

    prw = prw_ref[...]
    rolled = pltpu.roll(prw, 1, axis=0)
    row = lax.broadcasted_iota(jnp.int32, prw.shape, 0)
    prev = jnp.where(row == 0, prev_ref[...], rolled)
    prev_ref[...] = prw[tr - 1:tr, :]
    xs = prw + (prev - prw) * mu_ref[...]

    r = xs[:, :WIDTH]
    kr = xs[:, WIDTH:2 * WIDTH]
    vr = xs[:, 2 * WIDTH:3 * WIDTH]
    wa = xs[:, 3 * WIDTH:3 * WIDTH + 128]
    gd = xs[:, 3 * WIDTH + 128:]
    lane128 = lax.broadcasted_iota(jnp.int32, wa.shape, 1)
    wa_in = jnp.where(lane128 < 64, jnp.tanh(wa), wa).astype(BF16)
    lora = jnp.dot(wa_in, w2a2_ref[...], preferred_element_type=F32)
    w_log = _log_sigmoid(w0_ref[...] + lora[:, :WIDTH]) - 0.5
    lw = -jnp.exp(w_log)
    lr = jax.nn.sigmoid(a0_ref[...] + lora[:, WIDTH:])
    gate = jnp.dot(jax.nn.sigmoid(gd).astype(BF16), g2_ref[...], preferred_element_type=F32)

    bd = bd_ref[...]
    kk = kr * kk_ref[...]
    kk = kk / jnp.maximum(jnp.sqrt(_head_sums(kk * kk, bd)), 1e-12)
    k = kr * (1.0 + (lr - 1.0) * ka_ref[...])
    b = kk * lr

    ri = lax.broadcasted_iota(jnp.int32, (tr, tr), 0)
    ci = lax.broadcasted_iota(jnp.int32, (tr, tr), 1)
    tri = jnp.where((ci <= ri) & (ci // chunk == ri // chunk), 1.0, 0.0).astype(BF16)
    cs = jnp.zeros_like(lw)
    for part in _split_bf16(lw, N_SPLIT):
        cs = cs + jnp.dot(tri, part, preferred_element_type=F32)
    e_pos = jnp.exp(cs)
    e_neg = jnp.exp(-cs)
    at = (-kk * jnp.exp(cs - lw)).astype(BF16)
    rt = (r * e_pos).astype(BF16)
    bt = (b * e_neg).astype(BF16)
    kt = (k * e_neg).astype(BF16)
    vb = vr.astype(BF16)
    g_last, bh_rows, kh_rows = [], [], []
    for c in range(n_chunks):
        rows = slice(c * chunk, (c + 1) * chunk)
        last = cs[(c + 1) * chunk - 1:(c + 1) * chunk, :]
        to_end = jnp.exp(last - cs[rows, :])
        bh_rows.append((b[rows, :] * to_end).astype(BF16))
        kh_rows.append((k[rows, :] * to_end).astype(BF16))
        g_last.append(jnp.exp(last))

    c2 = 2 * chunk
    lane = lax.broadcasted_iota(jnp.int32, (chunk, 128), 1)
    even = lane < HEAD_DIM

    def stack(x, c, p):
        x = x[c * chunk:(c + 1) * chunk, p * 128:(p + 1) * 128]
        zero = jnp.zeros_like(x)
        return jnp.concatenate([jnp.where(even, x, zero), jnp.where(even, zero, x)], axis=0)

    rr = lax.broadcasted_iota(jnp.int32, (c2, c2), 0)
    cc = lax.broadcasted_iota(jnp.int32, (c2, c2), 1)
    same = (rr // chunk) == (cc // chunk)
    strict = same & (cc < rr)
    incl = same & (cc <= rr)
    eye = jnp.where(rr == cc, 1.0, 0.0)
    r128 = lax.broadcasted_iota(jnp.int32, (128, 128), 0)
    c128 = lax.broadcasted_iota(jnp.int32, (128, 128), 1)
    diag128 = r128 == c128

    insts = [(c, p) for c in range(n_chunks) for p in range(N_PAIRS)]
    a_s = [stack(at, c, p) for c, p in insts]
    r_s = [stack(rt, c, p) for c, p in insts]
    b_s = [stack(bt, c, p) for c, p in insts]
    k_s = [stack(kt, c, p) for c, p in insts]
    v_s = [stack(vb, c, p) for c, p in insts]
    bh_s = [stack(bh_rows[c], 0, p) for c, p in insts]
    kh_s = [stack(kh_rows[c], 0, p) for c, p in insts]
    wide = c2 % 128 == 0

    def pair(mm, lhs, r1, r2, axis=1):
        if not wide:
            return mm(lhs, r1), mm(lhs, r2)
        out = mm(lhs, jnp.concatenate([r1.astype(BF16), r2.astype(BF16)], axis=axis))
        return out[:, :out.shape[1] // 2], out[:, out.shape[1] // 2:]

    ar_s = [jnp.concatenate([a, rr_], axis=0) for a, rr_ in zip(a_s, r_s)]
    gbk = [pair(_mm_nt, x, y, z, axis=0) for x, y, z in zip(ar_s, b_s, k_s)]
    a_ab = [jnp.where(strict, gb[:c2], 0.0) for gb, _ in gbk]
    a_ak = [jnp.where(strict, gk[:c2], 0.0).astype(BF16) for _, gk in gbk]
    a_rb = [jnp.where(incl, gb[c2:], 0.0).astype(BF16) for gb, _ in gbk]
    a_rk = [jnp.where(incl, gk[c2:], 0.0).astype(BF16) for _, gk in gbk]
    x1 = [_mm(m, v) for m, v in zip(a_ak, v_s)]
    levels = int(math.log2(chunk))
    pw = [x.astype(BF16) for x in a_ab]
    inv = [eye + x for x in a_ab]
    pw = [_mm(x, x).astype(BF16) for x in pw]
    for j in range(1, levels):
        if j < levels - 1:
            sq_inc = [pair(_mm, x, x, i) for x, i in zip(pw, inv)]
            pw = [sq.astype(BF16) for sq, _ in sq_inc]
            inv = [i + inc for i, (_, inc) in zip(inv, sq_inc)]
        else:
            inv = [i + _mm(x, i) for x, i in zip(pw, inv)]
    inv = [i.astype(BF16) for i in inv]
    wm_u0 = [pair(_mm, i, a, x) for i, a, x in zip(inv, a_s, x1)]
    wm = [w.astype(BF16) for w, _ in wm_u0]
    u0 = [u.astype(BF16) for _, u in wm_u0]
    bh_wu = [pair(_mm_tn, bh, w, u) for bh, w, u in zip(bh_s, wm, u0)]
    rb_wu = [pair(_mm, m, w, u) for m, w, u in zip(a_rb, wm, u0)]
    mb = [x.astype(BF16) for x, _ in bh_wu]
    rm = [(rr_.astype(F32) + x).astype(BF16) for rr_, (x, _) in zip(r_s, rb_wu)]
    y0 = [x + _mm(m2, v) for (_, x), m2, v in zip(rb_wu, a_rk, v_s)]
    n0 = [x + _mm_tn(kh, v) for (_, x), kh, v in zip(bh_wu, kh_s, v_s)]

    states = [h_ref[p] for p in range(N_PAIRS)]
    for i, (c, p) in enumerate(insts):
        h = states[p]
        hb = h.astype(BF16)
        y = _mm(rm[i], hb) + y0[i]
        y_ref[c * chunk:(c + 1) * chunk, p * 128:(p + 1) * 128] = y[:chunk] + y[chunk:]
        g_col = jnp.sum(jnp.where(diag128, g_last[c][:, p * 128:(p + 1) * 128], 0.0), axis=1, keepdims=True)
        states[p] = g_col * h + _mm(mb[i], hb) + n0[i]
    for p in range(N_PAIRS):
        h_ref[p] = states[p]

    y = y_ref[...]
    mean = _head_sums(y, bd, 1) * (1.0 / HEAD_DIM)
    yc = y - mean
    var = _head_sums(yc * yc, bd, 1) * (1.0 / HEAD_DIM)
    yn = yc * lax.rsqrt(var + GN_EPS) * lng_ref[...] + lnb_ref[...]
    bonus = _head_sums(r * k * rk_ref[...], bd, 1) * vr
    o_ref[...] = ((yn + bonus) * gate).astype(o_ref.dtype)

    @pl.when(t == nt - 1)
    def _():
        sout_ref[0] = h_ref[...]


def _rwkv(prw, shift0, s0bd, wts, nseq, tr, chunk):
    n = prw.shape[0]
    tiles = n // (nseq * tr)
    row = lambda s, t: (s * tiles + t, 0)
    vec = lambda w: _const_spec((1, w))
    scratch = [pltpu.VMEM((1, RWKV_PROJ), F32), pltpu.VMEM((N_PAIRS, 128, 128), F32),
               pltpu.VMEM((tr, WIDTH), F32)]
    return pl.pallas_call(
        functools.partial(_rwkv_kernel, chunk=chunk),
        grid=(nseq, tiles),
        in_specs=[
            pl.BlockSpec((tr, RWKV_PROJ), row),
            pl.BlockSpec((1, 1, RWKV_PROJ), lambda s, t: (s, 0, 0)),
            pl.BlockSpec((1, N_PAIRS, 128, 128), lambda s, t: (s, 0, 0, 0)),
            vec(RWKV_PROJ), vec(WIDTH), vec(WIDTH),
            _const_spec((128, 2 * WIDTH)), _const_spec((128, WIDTH)),
            vec(WIDTH), vec(WIDTH), vec(WIDTH), vec(WIDTH), vec(WIDTH),
            _const_spec((WIDTH, WIDTH)),
        ],
        out_specs=(
            pl.BlockSpec((tr, WIDTH), row),
            pl.BlockSpec((1, N_PAIRS, 128, 128), lambda s, t: (s, 0, 0, 0)),
        ),
        out_shape=(
            jax.ShapeDtypeStruct((n, WIDTH), BF16),
            jax.ShapeDtypeStruct((nseq, N_PAIRS, 128, 128), F32),
        ),
        scratch_shapes=scratch,
        compiler_params=pltpu.CompilerParams(
            dimension_semantics=("arbitrary", "arbitrary"), vmem_limit_bytes=VMEM_LIMIT_BYTES),
        name="rwkv7_mix",
    )(prw, shift0, s0bd, *wts)


def _prep_rwkv_weights(mu, w0, w2, a0, a2, g2, k_k, k_a, r_k, ln_g, ln_b):
    z = jnp.zeros((64, WIDTH), F32)
    w2a2 = jnp.concatenate([jnp.concatenate([w2, z], axis=1), jnp.concatenate([z, a2], axis=1)], axis=0)
    seg = jnp.arange(WIDTH) // HEAD_DIM
    bd = (seg[:, None] == seg[None, :]).astype(BF16)
    return (mu[None], w0[None], a0[None], w2a2.astype(BF16), g2.astype(BF16), k_k[None], k_a[None],
            r_k.reshape(1, WIDTH), ln_g[None], ln_b[None], bd)


def _state_to_pairs(s):
    n = s.shape[0]
    ht = jnp.swapaxes(s, -1, -2).reshape(n, N_PAIRS, 2, HEAD_DIM, HEAD_DIM)
    z = jnp.zeros_like(ht[:, :, 0])
    top = jnp.concatenate([ht[:, :, 0], z], axis=-1)
    bot = jnp.concatenate([z, ht[:, :, 1]], axis=-1)
    return jnp.concatenate([top, bot], axis=-2)


def _pairs_to_state(hbd):
    n = hbd.shape[0]
    even = hbd[:, :, :HEAD_DIM, :HEAD_DIM]
    odd = hbd[:, :, HEAD_DIM:, HEAD_DIM:]
    ht = jnp.stack([even, odd], axis=2).reshape(n, N_HEADS, HEAD_DIM, HEAD_DIM)
    return jnp.swapaxes(ht, -1, -2)


def _attn_step_kernel(q_ref, kn_ref, vn_ref, lf_ref, ck_ref, cv_ref, clf_ref, o_ref, after_ref):
    t = q_ref.shape[0]
    past = ck_ref.shape[1]

    @pl.when(pl.program_id(0) == 0)
    def _():
        mi = lax.broadcasted_iota(jnp.int32, (past, past), 0)
        ji = lax.broadcasted_iota(jnp.int32, (past, past), 1)
        after_ref[...] = jnp.where(mi > ji, 1.0, 0.0).astype(BF16)

    after = after_ref[...]
    suffix = jnp.zeros(clf_ref.shape[1:], F32)
    for part in _split_bf16(clf_ref[0], N_SPLIT):
        suffix = suffix + jnp.dot(part, after, preferred_element_type=F32)
    suffix = suffix * LOG2E
    ri = lax.broadcasted_iota(jnp.int32, (t, t), 0)
    ci = lax.broadcasted_iota(jnp.int32, (t, t), 1)
    causal = ci <= ri
    tri = jnp.where(causal, 1.0, 0.0).astype(BF16)
    cn = jnp.zeros(lf_ref.shape, F32)
    for part in _split_bf16(lf_ref[...], N_SPLIT):
        cn = cn + jnp.dot(tri, part, preferred_element_type=F32)
    cn = cn * LOG2E
    cn_t = cn.T

    lane = lax.broadcasted_iota(jnp.int32, (t, 128), 1)
    for p in range(N_PAIRS):
        q_pair = q_ref[:, p * PAIR_LANES:p * PAIR_LANES + 128]
        lanes = slice(p * 128, (p + 1) * 128)
        k_past = ck_ref[0, :, lanes].astype(BF16)
        v_past = cv_ref[0, :, lanes].astype(BF16)
        k_new = kn_ref[:, lanes].astype(BF16)
        v_new = vn_ref[:, lanes].astype(BF16)
        outs = []
        for e in range(2):
            hd = 2 * p + e
            mine = (lane >= e * HEAD_DIM) & (lane < (e + 1) * HEAD_DIM)
            qm = jnp.where(mine, q_pair, jnp.zeros_like(q_pair))
            cq = cn[:, hd:hd + 1]
            s_past = lax.dot_general(qm, k_past, (((1,), (1,)), ((), ())), preferred_element_type=F32)
            s_past = s_past + cq + suffix[hd:hd + 1, :]
            s_new = lax.dot_general(qm, k_new, (((1,), (1,)), ((), ())), preferred_element_type=F32)
            s_new = jnp.where(causal, s_new + cq - cn_t[hd:hd + 1, :], NEG_BIG)
            m = jnp.maximum(jnp.max(s_past, axis=1, keepdims=True), jnp.max(s_new, axis=1, keepdims=True))
            p_past = jnp.exp2(s_past - m)
            p_new = jnp.exp2(s_new - m)
            denom = jnp.sum(p_past, axis=1, keepdims=True) + jnp.sum(p_new, axis=1, keepdims=True)
            pv = (jnp.dot(p_past.astype(BF16), v_past, preferred_element_type=F32)
                  + jnp.dot(p_new.astype(BF16), v_new, preferred_element_type=F32))
            outs.append(pv / denom)
        o_ref[:, lanes] = jnp.where(lane < HEAD_DIM, outs[0], outs[1]).astype(o_ref.dtype)


def _attention_step(qaug, kn, vn, logf, cache_k, cache_v, cache_logf_t, t):
    n = qaug.shape[0]
    nseq = n // t
    past = cache_k.shape[1]
    row = lambda b: (b, 0)
    return pl.pallas_call(
        _attn_step_kernel,
        grid=(nseq,),
        in_specs=[
            pl.BlockSpec((t, AUG_WIDTH), row),
            pl.BlockSpec((t, WIDTH), row),
            pl.BlockSpec((t, WIDTH), row),
            pl.BlockSpec((t, 128), row),
            pl.BlockSpec((1, past, WIDTH), lambda b: (b, 0, 0)),
            pl.BlockSpec((1, past, WIDTH), lambda b: (b, 0, 0)),
            pl.BlockSpec((1, N_HEADS, past), lambda b: (b, 0, 0)),
        ],
        out_specs=pl.BlockSpec((t, WIDTH), row),
        out_shape=jax.ShapeDtypeStruct((n, WIDTH), BF16),
        scratch_shapes=[pltpu.VMEM((past, past), BF16)],
        compiler_params=pltpu.CompilerParams(
            dimension_semantics=("arbitrary",), vmem_limit_bytes=VMEM_LIMIT_BYTES),
        name="fox_attention_step",
    )(qaug, kn, vn, logf, cache_k, cache_v, cache_logf_t)


def _out_kernel(x_ref, of_ref, orw_ref, wof_ref, worw_ref, g_ref, wg_ref, wu_ref, wo_ref, o_ref):
    mix = lax.dot_general(of_ref[0], wof_ref[...], (((0,), (0,)), ((), ())), preferred_element_type=F32)
    mix = mix + jnp.dot(orw_ref[...], worw_ref[...], preferred_element_type=F32)
    o_ref[...] = _swiglu_half_step(x_ref[...] + mix, g_ref[...], wg_ref, wu_ref, wo_ref)


def _out_ffn(x, o_fox_t, o_rwkv, wo_fox, wo_rwkv, g, wg, wu, wo, tm):
    n = x.shape[0]
    per_fox_tile = o_fox_t.shape[2] // tm
    row = lambda i: (i, 0)
    return pl.pallas_call(
        _out_kernel,
        grid=(n // tm,),
        in_specs=[
            pl.BlockSpec((tm, D_MODEL), row),
            pl.BlockSpec((1, WIDTH, tm), lambda i: (i // per_fox_tile, 0, i % per_fox_tile)),
            pl.BlockSpec((tm, WIDTH), row),
            _const_spec((WIDTH, D_MODEL)),
            _const_spec((WIDTH, D_MODEL)),
            _const_spec((1, D_MODEL)),
            _const_spec((D_MODEL, D_FF)),
            _const_spec((D_MODEL, D_FF)),
            _const_spec((D_FF, D_MODEL)),
        ],
        out_specs=pl.BlockSpec((tm, D_MODEL), row),
        out_shape=jax.ShapeDtypeStruct((n, D_MODEL), F32),
        compiler_params=pltpu.CompilerParams(
            dimension_semantics=("arbitrary",), vmem_limit_bytes=VMEM_LIMIT_BYTES),
        name="out_proj_ffn",
    )(x, o_fox_t, o_rwkv, wo_fox, wo_rwkv, g, wg, wu, wo)


PROMPT_TILE = 512
ATTN_QUERY_TILE = 1024
RWKV_TILE = 256
RWKV_CHUNK = 64


def kernel(x_prompt, x_sample, cache_fox_k, cache_fox_v, cache_fox_logf, state_rwkv, state_rwkv_shift, norm_ffn1_g, ffn1_w_in, ffn1_w_out, norm_mix_g, w_in, w_out, fox_b_f, fox_q_norm_g, fox_k_norm_g, rwkv_mu, rwkv_w0, rwkv_w2, rwkv_a0, rwkv_a2, rwkv_g2, rwkv_k_k, rwkv_k_a, rwkv_r_k, rwkv_ln_g, rwkv_ln_b, norm_ffn2_g, ffn2_w_in, ffn2_w_out):
    depth = norm_ffn1_g.shape[0]
    bp, seq, _ = x_prompt.shape
    bs, dec_seq, _ = x_sample.shape
    assert bp == 1, "the prompt path assumes a single stream"
    n_p, n_s = bp * seq, bs * dec_seq
    yp = x_prompt.reshape(n_p, D_MODEL)
    ys = x_sample.reshape(n_s, D_MODEL)
    prompt_states = ([], [], [], [], [])
    sample_states = ([], [], [], [], [])
    for l in range(depth):
        ffn1 = (norm_ffn1_g[l][None], ffn1_w_in[l, :, :D_FF].astype(BF16), ffn1_w_in[l, :, D_FF:].astype(BF16),
                ffn1_w_out[l].astype(BF16))
        ffn2 = (norm_ffn2_g[l][None], ffn2_w_in[l, :, :D_FF].astype(BF16), ffn2_w_in[l, :, D_FF:].astype(BF16),
                ffn2_w_out[l].astype(BF16))
        proj_w = _prep_proj_weights(w_in[l], fox_b_f[l], fox_q_norm_g[l], fox_k_norm_g[l])
        rwkv_w = _prep_rwkv_weights(rwkv_mu[l], rwkv_w0[l], rwkv_w2[l], rwkv_a0[l], rwkv_a2[l], rwkv_g2[l],
                                    rwkv_k_k[l], rwkv_k_a[l], rwkv_r_k[l], rwkv_ln_g[l], rwkv_ln_b[l])
        wo_fox = w_out[l, :WIDTH].astype(BF16)
        wo_rwkv = w_out[l, WIDTH:].astype(BF16)
        mix_g = norm_mix_g[l][None]

        x1 = _ffn(yp, *ffn1, PROMPT_TILE)
        qaug, kn, kaug, v, vt, logf, prw, base = _proj(x1, mix_g, *proj_w, PROMPT_TILE)
        o_fox_t = _attention(base[:, 0, :N_HEADS].T, qaug, kaug, vt, ATTN_QUERY_TILE, PROMPT_TILE)
        o_rwkv, s_pairs = _rwkv(prw, jnp.zeros((bp, 1, RWKV_PROJ), F32),
                                jnp.zeros((bp, N_PAIRS, 128, 128), F32), rwkv_w, bp, RWKV_TILE, RWKV_CHUNK)
        yp = _out_ffn(x1, o_fox_t, o_rwkv, wo_fox, wo_rwkv, *ffn2, PROMPT_TILE)
        for acc, st in zip(prompt_states, (
                kn.reshape(bp, seq, N_HEADS, HEAD_DIM), v.reshape(bp, seq, N_HEADS, HEAD_DIM),
                logf[:, :N_HEADS].reshape(bp, seq, N_HEADS), _pairs_to_state(s_pairs),
                prw.reshape(bp, seq, RWKV_PROJ)[:, -1:])):
            acc.append(st)

        x1 = _ffn(ys, *ffn1, n_s)
        qaug, kn, kaug, v, vt, logf, prw, base = _proj(x1, mix_g, *proj_w, n_s)
        past = cache_fox_k.shape[2]
        o_fox = _attention_step(
            qaug, kn, v, logf, cache_fox_k[l].reshape(bs, past, WIDTH), cache_fox_v[l].reshape(bs, past, WIDTH),
            jnp.swapaxes(cache_fox_logf[l], 1, 2), dec_seq)
        o_rwkv, s_pairs = _rwkv(prw, state_rwkv_shift[l], _state_to_pairs(state_rwkv[l]), rwkv_w,
                                bs, dec_seq, dec_seq)
        ys = _out_ffn(x1, o_fox.T[None], o_rwkv, wo_fox, wo_rwkv, *ffn2, n_s)
        for acc, st in zip(sample_states, (
                kn.reshape(bs, dec_seq, N_HEADS, HEAD_DIM), v.reshape(bs, dec_seq, N_HEADS, HEAD_DIM),
                logf[:, :N_HEADS].reshape(bs, dec_seq, N_HEADS), _pairs_to_state(s_pairs),
                prw.reshape(bs, dec_seq, RWKV_PROJ)[:, -1:])):
            acc.append(st)

    return (yp.reshape(bp, seq, D_MODEL), ys.reshape(bs, dec_seq, D_MODEL),
            *(jnp.stack(a) for a in prompt_states), *(jnp.stack(a) for a in sample_states))
```

```python
import functools
import math

import jax
import jax.numpy as jnp
from jax import lax
from jax.experimental import pallas as pl
from jax.experimental.pallas import tpu as pltpu

F32 = jnp.float32
BF16 = jnp.bfloat16

D_MODEL = 1024
D_FF = 2816
HEAD_DIM = 64
N_HEADS = 8
WIDTH = N_HEADS * HEAD_DIM
RWKV_PROJ = 1792
RMS_EPS = 1e-6
GN_EPS = 64e-5
LOG2E = math.log2(math.e)

VMEM_LIMIT_BYTES = 56 * 1024 * 1024
FF_CHUNK = 256


def _const_spec(shape):
    return pl.BlockSpec(shape, lambda *_: (0,) * len(shape), pipeline_mode=pl.Buffered(1))


def _rms(x, g):
    return x * lax.rsqrt(jnp.mean(x * x, axis=-1, keepdims=True) + RMS_EPS) * g


def _swiglu_half_step(x, g, wi_ref, wo_ref):
    h = _rms(x, g).astype(BF16)
    acc = x
    for c0 in range(0, D_FF, FF_CHUNK):
        c1 = min(c0 + FF_CHUNK, D_FF)
        gate = jnp.dot(h, wi_ref[:, c0:c1], preferred_element_type=F32)
        up = jnp.dot(h, wi_ref[:, D_FF + c0:D_FF + c1], preferred_element_type=F32)
        a = (gate * jax.nn.sigmoid(gate) * up).astype(BF16)
        acc = acc + 0.5 * jnp.dot(a, wo_ref[c0:c1, :], preferred_element_type=F32)
    return acc


def _ffn_kernel(x_ref, g_ref, wi_ref, wo_ref, o_ref):
    o_ref[...] = _swiglu_half_step(x_ref[...], g_ref[...], wi_ref, wo_ref)


def _ffn(x, g, wi, wo, tm):
    n = x.shape[0]
    return pl.pallas_call(
        _ffn_kernel,
        grid=(n // tm,),
        in_specs=[
            pl.BlockSpec((tm, D_MODEL), lambda i: (i, 0)),
            _const_spec((1, D_MODEL)),
            _const_spec((D_MODEL, 2 * D_FF)),
            _const_spec((D_FF, D_MODEL)),
        ],
        out_specs=pl.BlockSpec((tm, D_MODEL), lambda i: (i, 0)),
        out_shape=jax.ShapeDtypeStruct((n, D_MODEL), F32),
        compiler_params=pltpu.CompilerParams(
            dimension_semantics=("arbitrary",), vmem_limit_bytes=VMEM_LIMIT_BYTES),
        name="ffn",
    )(x, g, wi, wo)


PROJ_COLS = 3 * WIDTH + RWKV_PROJ + 128
V_ROWS = HEAD_DIM + 16
N_SPLIT = 3
PAIR_LANES = 256
AUG_WIDTH = (N_HEADS // 2) * PAIR_LANES


def _split_bf16(x, n):
    parts = []
    for _ in range(n):
        p = x.astype(BF16)
        parts.append(p)
        x = x - p.astype(F32)
    return parts


def _head_sums(x, bd, terms=2):
    return sum(jnp.dot(part, bd, preferred_element_type=F32) for part in _split_bf16(x, terms))


def _log_sigmoid(x):
    return jnp.minimum(x, 0.0) - jnp.log1p(jnp.exp(-jnp.abs(x)))


def _proj_kernel(x_ref, g_ref, w_ref, bf_ref, qg_ref, kg_ref, bd_ref, eq_ref, ek_ref, oneq_ref, onek_ref,
                 qaug_ref, k_ref, kaug_ref, v_ref, vt_ref, logf_ref, prw_ref, base_ref, carry_ref):
    i = pl.program_id(0)
    tm = x_ref.shape[0]

    @pl.when(i == 0)
    def _():
        carry_ref[...] = jnp.zeros_like(carry_ref)

    h = _rms(x_ref[...], g_ref[...]).astype(BF16)
    proj = jnp.dot(h, w_ref[...], preferred_element_type=F32)
    q = proj[:, :WIDTH]
    k = proj[:, WIDTH:2 * WIDTH]
    v = proj[:, 2 * WIDTH:3 * WIDTH]
    prw_ref[...] = proj[:, 3 * WIDTH:3 * WIDTH + RWKV_PROJ]
    f = proj[:, 3 * WIDTH + RWKV_PROJ:]

    bd = bd_ref[...]
    qn = q * lax.rsqrt(_head_sums(q * q, bd, 1) * (1.0 / HEAD_DIM) + RMS_EPS) * qg_ref[...]
    kn = k * lax.rsqrt(_head_sums(k * k, bd, 1) * (1.0 / HEAD_DIM) + RMS_EPS) * kg_ref[...]
    k_ref[...] = kn
    v_ref[...] = v
    v_t = v.T.astype(BF16)
    for hd in range(N_HEADS):
        vt_ref[0, hd * V_ROWS:hd * V_ROWS + HEAD_DIM, :] = v_t[hd * HEAD_DIM:(hd + 1) * HEAD_DIM, :]
        vt_ref[0, hd * V_ROWS + HEAD_DIM:(hd + 1) * V_ROWS, :] = jnp.ones((V_ROWS - HEAD_DIM, tm), BF16)

    lane = lax.broadcasted_iota(jnp.int32, f.shape, 1)
    logf = jnp.where(lane < N_HEADS, _log_sigmoid(f + bf_ref[...]), 0.0)
    logf_ref[...] = logf

    row = lax.broadcasted_iota(jnp.int32, (tm, tm), 0)
    col = lax.broadcasted_iota(jnp.int32, (tm, tm), 1)
    tri = jnp.where(col <= row, 1.0, 0.0).astype(BF16)
    cl = jnp.zeros_like(logf)
    for part in _split_bf16(logf, N_SPLIT):
        cl = cl + jnp.dot(tri, part, preferred_element_type=F32)
    c2 = cl * LOG2E
    base_ref[0] = jnp.broadcast_to(carry_ref[...], base_ref.shape[1:])
    carry_ref[...] = carry_ref[...] + c2[tm - 1:tm, :]

    terms = _split_bf16(c2, N_SPLIT)
    parts = sum(pltpu.roll(t.astype(F32), N_HEADS * s, axis=1) if s else t.astype(F32)
                for s, t in enumerate(terms)).astype(BF16)
    aug_q = (jnp.dot(parts, eq_ref[...], preferred_element_type=F32) + oneq_ref[...]).astype(BF16)
    aug_k = (jnp.dot(parts, ek_ref[...], preferred_element_type=F32) + onek_ref[...]).astype(BF16)
    qs = (qn * (HEAD_DIM ** -0.5 * LOG2E)).astype(BF16)
    ks = kn.astype(BF16)
    for p in range(N_HEADS // 2):
        qaug_ref[:, p * PAIR_LANES:p * PAIR_LANES + 128] = qs[:, p * 128:(p + 1) * 128]
        qaug_ref[:, p * PAIR_LANES + 128:(p + 1) * PAIR_LANES] = aug_q[:, p * 128:(p + 1) * 128]
        kaug_ref[:, p * PAIR_LANES:p * PAIR_LANES + 128] = ks[:, p * 128:(p + 1) * 128]
        kaug_ref[:, p * PAIR_LANES + 128:(p + 1) * PAIR_LANES] = aug_k[:, p * 128:(p + 1) * 128]


def _carrier_constants():
    eq = [[0.0] * WIDTH for _ in range(128)]
    ek = [[0.0] * WIDTH for _ in range(128)]
    oneq = [0.0] * WIDTH
    onek = [0.0] * WIDTH
    for hd in range(N_HEADS):
        off = (hd // 2) * 128 + (hd % 2) * 8
        for s in range(N_SPLIT):
            eq[s * N_HEADS + hd][off + s] = 1.0
            ek[s * N_HEADS + hd][off + N_SPLIT + s] = -1.0
            oneq[off + N_SPLIT + s] = 1.0
            onek[off + s] = 1.0
    return (jnp.array(eq, BF16), jnp.array(ek, BF16), jnp.array([oneq], F32), jnp.array([onek], F32))


def _prep_proj_weights(w_in, fox_b_f, q_norm_g, k_norm_g):
    fox_cols = 3 * WIDTH
    w_f = jnp.pad(w_in[:, fox_cols:fox_cols + N_HEADS], ((0, 0), (0, 128 - N_HEADS)))
    w_all = jnp.concatenate([w_in[:, :fox_cols], w_in[:, fox_cols + N_HEADS:], w_f], axis=1).astype(BF16)
    bf_pad = jnp.pad(fox_b_f, (0, 128 - N_HEADS))[None]
    return w_all, bf_pad, jnp.tile(q_norm_g, N_HEADS)[None], jnp.tile(k_norm_g, N_HEADS)[None]


def _proj(x, g, w_all, bf_pad, qg, kg, tm):
    n = x.shape[0]
    nblk = n // tm
    seg = jnp.arange(WIDTH) // HEAD_DIM
    bd = (seg[:, None] == seg[None, :]).astype(BF16)
    eq, ek, oneq, onek = _carrier_constants()
    row = lambda i: (i, 0)
    out_shape = (
        jax.ShapeDtypeStruct((n, AUG_WIDTH), BF16),
        jax.ShapeDtypeStruct((n, WIDTH), F32),
        jax.ShapeDtypeStruct((n, AUG_WIDTH), BF16),
        jax.ShapeDtypeStruct((n, WIDTH), F32),
        jax.ShapeDtypeStruct((nblk, N_HEADS * V_ROWS, tm), BF16),
        jax.ShapeDtypeStruct((n, 128), F32),
        jax.ShapeDtypeStruct((n, RWKV_PROJ), F32),
        jax.ShapeDtypeStruct((nblk, 8, 128), F32),
    )
    out_specs = (
        pl.BlockSpec((tm, AUG_WIDTH), row),
        pl.BlockSpec((tm, WIDTH), row),
        pl.BlockSpec((tm, AUG_WIDTH), row),
        pl.BlockSpec((tm, WIDTH), row),
        pl.BlockSpec((1, N_HEADS * V_ROWS, tm), lambda i: (i, 0, 0)),
        pl.BlockSpec((tm, 128), row),
        pl.BlockSpec((tm, RWKV_PROJ), row),
        pl.BlockSpec((1, 8, 128), lambda i: (i, 0, 0)),
    )
    return pl.pallas_call(
        _proj_kernel,
        grid=(nblk,),
        in_specs=[
            pl.BlockSpec((tm, D_MODEL), row),
            _const_spec((1, D_MODEL)),
            _const_spec((D_MODEL, PROJ_COLS)),
            _const_spec((1, 128)),
            _const_spec((1, WIDTH)),
            _const_spec((1, WIDTH)),
            _const_spec((WIDTH, WIDTH)),
            _const_spec((128, WIDTH)),
            _const_spec((128, WIDTH)),
            _const_spec((1, WIDTH)),
            _const_spec((1, WIDTH)),
        ],
        out_specs=out_specs,
        out_shape=out_shape,
        scratch_shapes=[pltpu.VMEM((1, 128), F32)],
        compiler_params=pltpu.CompilerParams(
            dimension_semantics=("arbitrary",), vmem_limit_bytes=VMEM_LIMIT_BYTES),
        name="mix_proj",
    )(x, g, w_all, bf_pad, qg, kg, bd, eq, ek, oneq, onek)


NEG_BIG = -1e30
STRIP = 256


def _attn_kernel(base_ref, q_ref, k_ref, vt_ref, o_ref,
                 qt_ref, s0_ref, s1_ref, x0_ref, x1_ref, p0_ref, p1_ref, m_ref, alpha_ref, acc_ref, *, tk):
    pair = pl.program_id(0)
    qi = pl.program_id(1)
    tq = q_ref.shape[0]
    sub = tq // tk
    n_steps = (qi + 1) * sub
    assert sub in (1, 2), "the first (peeled) trip must hold every diagonal block"
    strips = [(e, h) for e in range(2) for h in range(tq // STRIP)]

    def cols(e, h):
        return slice(e * tq + h * STRIP, e * tq + (h + 1) * STRIP)

    def strip_id(e, h):
        return e * (tq // STRIP) + h

    lane = lax.broadcasted_iota(jnp.int32, (tq, PAIR_LANES), 1)
    q = q_ref[...].astype(F32)
    for e in range(2):
        f0 = e * HEAD_DIM
        c0 = 128 + e * 8
        keep = ((lane >= f0) & (lane < f0 + HEAD_DIM)) | ((lane >= c0) & (lane < c0 + 2 * N_SPLIT))
        qt_ref[:, e * tq:(e + 1) * tq] = jnp.where(keep, q, 0.0).T.astype(BF16)

    def block_of(step):
        return jnp.where(step < sub, qi * sub + step, step - sub)

    def offset(step, e, h):
        hd = 2 * pair + e
        d = base_ref[hd, qi * sub + (h * STRIP) // tk] - base_ref[hd, block_of(step)]
        return jnp.where(step < n_steps, d, NEG_BIG)

    r = lax.broadcasted_iota(jnp.int32, (tk, STRIP), 0)
    c = lax.broadcasted_iota(jnp.int32, (tk, STRIP), 1)

    def scores(step, s_ref, smax_ref, e, h, diag=None):
        blk = block_of(jnp.minimum(step, n_steps))
        kc = k_ref[pl.ds(pl.multiple_of(blk * tk, tk), tk), :]
        s = jnp.dot(kc, qt_ref[:, cols(e, h)], preferred_element_type=F32)
        if diag is not None:
            s = jnp.where(r + diag * tk <= c + h * STRIP, s, NEG_BIG)
        s_ref[strip_id(e, h)] = s
        smax_ref[:, cols(e, h)] = jnp.max(s, axis=0, keepdims=True)

    def softmax(step, s_ref, smax_ref, p_ref):
        for e, h in strips:
            cs = cols(e, h)
            d = offset(step, e, h)
            m_old = m_ref[:, cs]
            m_new = jnp.maximum(m_old, smax_ref[:, cs] + d)
            m_ref[:, cs] = m_new
            alpha_ref[:, cs] = jnp.exp2(m_old - m_new)
            p_ref[strip_id(e, h)] = jnp.exp2((s_ref[strip_id(e, h)] - (m_new - d)).astype(BF16))

    def accumulate(step, p_ref, e, h):
        vt = vt_ref[block_of(step), e * V_ROWS:(e + 1) * V_ROWS, :]
        pv = jnp.dot(vt, p_ref[strip_id(e, h)], preferred_element_type=F32)
        hs = slice(h * STRIP, (h + 1) * STRIP)
        acc_ref[e, :, hs] = alpha_ref[:, cols(e, h)] * acc_ref[e, :, hs] + pv

    def two_steps(k, first=False):
        for e, h in strips:
            if not first:
                accumulate(k - 1, p1_ref, e, h)
            scores(k + 1, s1_ref, x1_ref, e, h, diag=1 if (first and sub > 1) else None)
        softmax(k, s0_ref, x0_ref, p0_ref)
        for e, h in strips:
            accumulate(k, p0_ref, e, h)
            scores(k + 2, s0_ref, x0_ref, e, h)
        softmax(k + 1, s1_ref, x1_ref, p1_ref)

    m_ref[...] = jnp.full(m_ref.shape, NEG_BIG, F32)
    acc_ref[...] = jnp.zeros_like(acc_ref)
    for e, h in strips:
        scores(0, s0_ref, x0_ref, e, h, diag=0)
    two_steps(0, first=True)

    def trip(it, carry):
        two_steps(2 * it)
        return carry

    n_trips = (n_steps + 1) // 2
    lax.fori_loop(1, n_trips, trip, 0)
    for e, h in strips:
        accumulate(2 * n_trips - 1, p1_ref, e, h)
    for e in range(2):
        acc = acc_ref[e]
        o_ref[0, e * HEAD_DIM:(e + 1) * HEAD_DIM, :] = (
            acc[:HEAD_DIM] / acc[HEAD_DIM:HEAD_DIM + 1]).astype(o_ref.dtype)


def _attention(base, qaug, kaug, vt, tq, tk):
    n = qaug.shape[0]
    grid_spec = pltpu.PrefetchScalarGridSpec(
        num_scalar_prefetch=1,
        grid=(N_HEADS // 2, n // tq),
        in_specs=[
            pl.BlockSpec((tq, PAIR_LANES), lambda p, i, b: (i, p)),
            pl.BlockSpec((n, PAIR_LANES), lambda p, i, b: (0, p)),
            pl.BlockSpec((n // tk, 2 * V_ROWS, tk), lambda p, i, b: (0, p, 0)),
        ],
        out_specs=pl.BlockSpec((1, 2 * HEAD_DIM, tq), lambda p, i, b: (i, p, 0)),
        scratch_shapes=[
            pltpu.VMEM((PAIR_LANES, 2 * tq), BF16),
            pltpu.VMEM((2 * tq // STRIP, tk, STRIP), F32),
            pltpu.VMEM((2 * tq // STRIP, tk, STRIP), F32),
            pltpu.VMEM((1, 2 * tq), F32),
            pltpu.VMEM((1, 2 * tq), F32),
            pltpu.VMEM((2 * tq // STRIP, tk, STRIP), BF16),
            pltpu.VMEM((2 * tq // STRIP, tk, STRIP), BF16),
            pltpu.VMEM((1, 2 * tq), F32),
            pltpu.VMEM((1, 2 * tq), F32),
            pltpu.VMEM((2, V_ROWS, tq), F32),
        ],
    )
    return pl.pallas_call(
        functools.partial(_attn_kernel, tk=tk),
        grid_spec=grid_spec,
        out_shape=jax.ShapeDtypeStruct((n // tq, WIDTH, tq), BF16),
        compiler_params=pltpu.CompilerParams(
            dimension_semantics=("arbitrary", "arbitrary"), vmem_limit_bytes=VMEM_LIMIT_BYTES),
        name="fox_attention",
    )(base, qaug, kaug, vt)


N_PAIRS = N_HEADS // 2


def _mm(a, b):
    return jnp.dot(a.astype(BF16), b.astype(BF16), preferred_element_type=F32)


def _mm_nt(a, b):
    return lax.dot_general(a.astype(BF16), b.astype(BF16), (((1,), (1,)), ((), ())), preferred_element_type=F32)


def _mm_tn(a, b):
    return lax.dot_general(a.astype(BF16), b.astype(BF16), (((0,), (0,)), ((), ())), preferred_element_type=F32)


def _rwkv_kernel(prw_ref, shift0_ref, s0_ref, mu_ref, w0_ref, a0_ref, w2a2_ref, g2_ref, kk_ref, ka_ref, rk_ref,
                 lng_ref, lnb_ref, bd_ref,
                 o_ref, sout_ref,
                 prev_ref, h_ref, y_ref,
                 *, chunk):
    t = pl.program_id(1)
    nt = pl.num_programs(1)
    tr = prw_ref.shape[0]
    n_chunks = tr // chunk

    @pl.when(t == 0)
    def _():
        prev_ref[...] = shift0_ref[0]
        h_ref[...] = s0_ref[0]

    prw = prw_ref[...]
    rolled = pltpu.roll(prw, 1, axis=0)
    row = lax.broadcasted_iota(jnp.int32, prw.shape, 0)
    prev = jnp.where(row == 0, prev_ref[...], rolled)
    prev_ref[...] = prw[tr - 1:tr, :]
    xs = prw + (prev - prw) * mu_ref[...]

    r = xs[:, :WIDTH]
    kr = xs[:, WIDTH:2 * WIDTH]
    vr = xs[:, 2 * WIDTH:3 * WIDTH]
    wa = xs[:, 3 * WIDTH:3 * WIDTH + 128]
    gd = xs[:, 3 * WIDTH + 128:]
    lane128 = lax.broadcasted_iota(jnp.int32, wa.shape, 1)
    wa_in = jnp.where(lane128 < 64, jnp.tanh(wa), wa).astype(BF16)
    lora = jnp.dot(wa_in, w2a2_ref[...], preferred_element_type=F32)
    w_log = _log_sigmoid(w0_ref[...] + lora[:, :WIDTH]) - 0.5
    lw = -jnp.exp(w_log)
    lr = jax.nn.sigmoid(a0_ref[...] + lora[:, WIDTH:])
    gate = jnp.dot(jax.nn.sigmoid(gd).astype(BF16), g2_ref[...], preferred_element_type=F32)

    bd = bd_ref[...]
    kk = kr * kk_ref[...]
    kk = kk / jnp.maximum(jnp.sqrt(_head_sums(kk * kk, bd)), 1e-12)
    k = kr * (1.0 + (lr - 1.0) * ka_ref[...])
    b = kk * lr

    ri = lax.broadcasted_iota(jnp.int32, (tr, tr), 0)
    ci = lax.broadcasted_iota(jnp.int32, (tr, tr), 1)
    tri = jnp.where((ci <= ri) & (ci // chunk == ri // chunk), 1.0, 0.0).astype(BF16)
    cs = jnp.zeros_like(lw)
    for part in _split_bf16(lw, N_SPLIT):
        cs = cs + jnp.dot(tri, part, preferred_element_type=F32)
    e_pos = jnp.exp(cs)
    e_neg = jnp.exp(-cs)
    at = (-kk * jnp.exp(cs - lw)).astype(BF16)
    rt = (r * e_pos).astype(BF16)
    bt = (b * e_neg).astype(BF16)
    kt = (k * e_neg).astype(BF16)
    vb = vr.astype(BF16)
    g_last, bh_rows, kh_rows = [], [], []
    for c in range(n_chunks):
        rows = slice(c * chunk, (c + 1) * chunk)
        last = cs[(c + 1) * chunk - 1:(c + 1) * chunk, :]
        to_end = jnp.exp(last - cs[rows, :])
        bh_rows.append((b[rows, :] * to_end).astype(BF16))
        kh_rows.append((k[rows, :] * to_end).astype(BF16))
        g_last.append(jnp.exp(last))

    c2 = 2 * chunk
    lane = lax.broadcasted_iota(jnp.int32, (chunk, 128), 1)
    even = lane < HEAD_DIM

    def stack(x, c, p):
        x = x[c * chunk:(c + 1) * chunk, p * 128:(p + 1) * 128]
        zero = jnp.zeros_like(x)
        return jnp.concatenate([jnp.where(even, x, zero), jnp.where(even, zero, x)], axis=0)

    rr = lax.broadcasted_iota(jnp.int32, (c2, c2), 0)
    cc = lax.broadcasted_iota(jnp.int32, (c2, c2), 1)
    same = (rr // chunk) == (cc // chunk)
    strict = same & (cc < rr)
    incl = same & (cc <= rr)
    eye = jnp.where(rr == cc, 1.0, 0.0)
    r128 = lax.broadcasted_iota(jnp.int32, (128, 128), 0)
    c128 = lax.broadcasted_iota(jnp.int32, (128, 128), 1)
    diag128 = r128 == c128

    insts = [(c, p) for c in range(n_chunks) for p in range(N_PAIRS)]
    a_s = [stack(at, c, p) for c, p in insts]
    r_s = [stack(rt, c, p) for c, p in insts]
    b_s = [stack(bt, c, p) for c, p in insts]
    k_s = [stack(kt, c, p) for c, p in insts]
    v_s = [stack(vb, c, p) for c, p in insts]
    bh_s = [stack(bh_rows[c], 0, p) for c, p in insts]
    kh_s = [stack(kh_rows[c], 0, p) for c, p in insts]
    wide = c2 % 128 == 0

    def pair(mm, lhs, r1, r2, axis=1):
        if not wide:
            return mm(lhs, r1), mm(lhs, r2)
        out = mm(lhs, jnp.concatenate([r1.astype(BF16), r2.astype(BF16)], axis=axis))
        return out[:, :out.shape[1] // 2], out[:, out.shape[1] // 2:]

    ar_s = [jnp.concatenate([a, rr_], axis=0) for a, rr_ in zip(a_s, r_s)]
    gbk = [pair(_mm_nt, x, y, z, axis=0) for x, y, z in zip(ar_s, b_s, k_s)]
    a_ab = [jnp.where(strict, gb[:c2], 0.0) for gb, _ in gbk]
    a_ak = [jnp.where(strict, gk[:c2], 0.0).astype(BF16) for _, gk in gbk]
    a_rb = [jnp.where(incl, gb[c2:], 0.0).astype(BF16) for gb, _ in gbk]
    a_rk = [jnp.where(incl, gk[c2:], 0.0).astype(BF16) for _, gk in gbk]
    x1 = [_mm(m, v) for m, v in zip(a_ak, v_s)]
    levels = int(math.log2(chunk))
    pw = [x.astype(BF16) for x in a_ab]
    inv = [eye + x for x in a_ab]
    pw = [_mm(x, x).astype(BF16) for x in pw]
    for j in range(1, levels):
        if j < levels - 1:
            sq_inc = [pair(_mm, x, x, i) for x, i in zip(pw, inv)]
            pw = [sq.astype(BF16) for sq, _ in sq_inc]
            inv = [i + inc for i, (_, inc) in zip(inv, sq_inc)]
        else:
            inv = [i + _mm(x, i) for x, i in zip(pw, inv)]
    inv = [i.astype(BF16) for i in inv]
    wm_u0 = [pair(_mm, i, a, x) for i, a, x in zip(inv, a_s, x1)]
    wm = [w.astype(BF16) for w, _ in wm_u0]
    u0 = [u.astype(BF16) for _, u in wm_u0]
    bh_wu = [pair(_mm_tn, bh, w, u) for bh, w, u in zip(bh_s, wm, u0)]
    rb_wu = [pair(_mm, m, w, u) for m, w, u in zip(a_rb, wm, u0)]
    mb = [x.astype(BF16) for x, _ in bh_wu]
    rm = [(rr_.astype(F32) + x).astype(BF16) for rr_, (x, _) in zip(r_s, rb_wu)]
    y0 = [x + _mm(m2, v) for (_, x), m2, v in zip(rb_wu, a_rk, v_s)]
    n0 = [x + _mm_tn(kh, v) for (_, x), kh, v in zip(bh_wu, kh_s, v_s)]

    states = [h_ref[p] for p in range(N_PAIRS)]
    for i, (c, p) in enumerate(insts):
        h = states[p]
        hb = h.astype(BF16)
        y = _mm(rm[i], hb) + y0[i]
        y_ref[c * chunk:(c + 1) * chunk, p * 128:(p + 1) * 128] = y[:chunk] + y[chunk:]
        g_col = jnp.sum(jnp.where(diag128, g_last[c][:, p * 128:(p + 1) * 128], 0.0), axis=1, keepdims=True)
        states[p] = g_col * h + _mm(mb[i], hb) + n0[i]
    for p in range(N_PAIRS):
        h_ref[p] = states[p]

    y = y_ref[...]
    mean = _head_sums(y, bd, 1) * (1.0 / HEAD_DIM)
    yc = y - mean
    var = _head_sums(yc * yc, bd, 1) * (1.0 / HEAD_DIM)
    yn = yc * lax.rsqrt(var + GN_EPS) * lng_ref[...] + lnb_ref[...]
    bonus = _head_sums(r * k * rk_ref[...], bd, 1) * vr
    o_ref[...] = ((yn + bonus) * gate).astype(o_ref.dtype)

    @pl.when(t == nt - 1)
    def _():
        sout_ref[0] = h_ref[...]


def _rwkv(prw, shift0, s0bd, wts, nseq, tr, chunk):
    n = prw.shape[0]
    tiles = n // (nseq * tr)
    row = lambda s, t: (s * tiles + t, 0)
    vec = lambda w: _const_spec((1, w))
    scratch = [pltpu.VMEM((1, RWKV_PROJ), F32), pltpu.VMEM((N_PAIRS, 128, 128), F32),
               pltpu.VMEM((tr, WIDTH), F32)]
    return pl.pallas_call(
        functools.partial(_rwkv_kernel, chunk=chunk),
        grid=(nseq, tiles),
        in_specs=[
            pl.BlockSpec((tr, RWKV_PROJ), row),
            pl.BlockSpec((1, 1, RWKV_PROJ), lambda s, t: (s, 0, 0)),
            pl.BlockSpec((1, N_PAIRS, 128, 128), lambda s, t: (s, 0, 0, 0)),
            vec(RWKV_PROJ), vec(WIDTH), vec(WIDTH),
            _const_spec((128, 2 * WIDTH)), _const_spec((128, WIDTH)),
            vec(WIDTH), vec(WIDTH), vec(WIDTH), vec(WIDTH), vec(WIDTH),
            _const_spec((WIDTH, WIDTH)),
        ],
        out_specs=(
            pl.BlockSpec((tr, WIDTH), row),
            pl.BlockSpec((1, N_PAIRS, 128, 128), lambda s, t: (s, 0, 0, 0)),
        ),
        out_shape=(
            jax.ShapeDtypeStruct((n, WIDTH), BF16),
            jax.ShapeDtypeStruct((nseq, N_PAIRS, 128, 128), F32),
        ),
        scratch_shapes=scratch,
        compiler_params=pltpu.CompilerParams(
            dimension_semantics=("arbitrary", "arbitrary"), vmem_limit_bytes=VMEM_LIMIT_BYTES),
        name="rwkv7_mix",
    )(prw, shift0, s0bd, *wts)


def _prep_rwkv_weights(mu, w0, w2, a0, a2, g2, k_k, k_a, r_k, ln_g, ln_b):
    z = jnp.zeros((64, WIDTH), F32)
    w2a2 = jnp.concatenate([jnp.concatenate([w2, z], axis=1), jnp.concatenate([z, a2], axis=1)], axis=0)
    seg = jnp.arange(WIDTH) // HEAD_DIM
    bd = (seg[:, None] == seg[None, :]).astype(BF16)
    return (mu[None], w0[None], a0[None], w2a2.astype(BF16), g2.astype(BF16), k_k[None], k_a[None],
            r_k.reshape(1, WIDTH), ln_g[None], ln_b[None], bd)


def _state_to_pairs(s):
    n = s.shape[0]
    ht = jnp.swapaxes(s, -1, -2).reshape(n, N_PAIRS, 2, HEAD_DIM, HEAD_DIM)
    z = jnp.zeros_like(ht[:, :, 0])
    top = jnp.concatenate([ht[:, :, 0], z], axis=-1)
    bot = jnp.concatenate([z, ht[:, :, 1]], axis=-1)
    return jnp.concatenate([top, bot], axis=-2)


def _pairs_to_state(hbd):
    n = hbd.shape[0]
    even = hbd[:, :, :HEAD_DIM, :HEAD_DIM]
    odd = hbd[:, :, HEAD_DIM:, HEAD_DIM:]
    ht = jnp.stack([even, odd], axis=2).reshape(n, N_HEADS, HEAD_DIM, HEAD_DIM)
    return jnp.swapaxes(ht, -1, -2)


def _attn_step_kernel(q_ref, kn_ref, vn_ref, lf_ref, ck_ref, cv_ref, clf_ref, o_ref, after_ref):
    t = q_ref.shape[0]
    past = ck_ref.shape[1]

    @pl.when(pl.program_id(0) == 0)
    def _():
        mi = lax.broadcasted_iota(jnp.int32, (past, past), 0)
        ji = lax.broadcasted_iota(jnp.int32, (past, past), 1)
        after_ref[...] = jnp.where(mi > ji, 1.0, 0.0).astype(BF16)

    after = after_ref[...]
    suffix = jnp.zeros(clf_ref.shape[1:], F32)
    for part in _split_bf16(clf_ref[0], N_SPLIT):
        suffix = suffix + jnp.dot(part, after, preferred_element_type=F32)
    suffix = suffix * LOG2E
    ri = lax.broadcasted_iota(jnp.int32, (t, t), 0)
    ci = lax.broadcasted_iota(jnp.int32, (t, t), 1)
    causal = ci <= ri
    tri = jnp.where(causal, 1.0, 0.0).astype(BF16)
    cn = jnp.zeros(lf_ref.shape, F32)
    for part in _split_bf16(lf_ref[...], N_SPLIT):
        cn = cn + jnp.dot(tri, part, preferred_element_type=F32)
    cn = cn * LOG2E
    cn_t = cn.T

    lane = lax.broadcasted_iota(jnp.int32, (t, 128), 1)
    for p in range(N_PAIRS):
        q_pair = q_ref[:, p * PAIR_LANES:p * PAIR_LANES + 128]
        lanes = slice(p * 128, (p + 1) * 128)
        k_past = ck_ref[0, :, lanes].astype(BF16)
        v_past = cv_ref[0, :, lanes].astype(BF16)
        k_new = kn_ref[:, lanes].astype(BF16)
        v_new = vn_ref[:, lanes].astype(BF16)
        outs = []
        for e in range(2):
            hd = 2 * p + e
            mine = (lane >= e * HEAD_DIM) & (lane < (e + 1) * HEAD_DIM)
            qm = jnp.where(mine, q_pair, jnp.zeros_like(q_pair))
            cq = cn[:, hd:hd + 1]
            s_past = lax.dot_general(qm, k_past, (((1,), (1,)), ((), ())), preferred_element_type=F32)
            s_past = s_past + cq + suffix[hd:hd + 1, :]
            s_new = lax.dot_general(qm, k_new, (((1,), (1,)), ((), ())), preferred_element_type=F32)
            s_new = jnp.where(causal, s_new + cq - cn_t[hd:hd + 1, :], NEG_BIG)
            m = jnp.maximum(jnp.max(s_past, axis=1, keepdims=True), jnp.max(s_new, axis=1, keepdims=True))
            p_past = jnp.exp2(s_past - m)
            p_new = jnp.exp2(s_new - m)
            denom = jnp.sum(p_past, axis=1, keepdims=True) + jnp.sum(p_new, axis=1, keepdims=True)
            pv = (jnp.dot(p_past.astype(BF16), v_past, preferred_element_type=F32)
                  + jnp.dot(p_new.astype(BF16), v_new, preferred_element_type=F32))
            outs.append(pv / denom)
        o_ref[:, lanes] = jnp.where(lane < HEAD_DIM, outs[0], outs[1]).astype(o_ref.dtype)


def _attention_step(qaug, kn, vn, logf, cache_k, cache_v, cache_logf_t, t):
    n = qaug.shape[0]
    nseq = n // t
    past = cache_k.shape[1]
    row = lambda b: (b, 0)
    return pl.pallas_call(
        _attn_step_kernel,
        grid=(nseq,),
        in_specs=[
            pl.BlockSpec((t, AUG_WIDTH), row),
            pl.BlockSpec((t, WIDTH), row),
            pl.BlockSpec((t, WIDTH), row),
            pl.BlockSpec((t, 128), row),
            pl.BlockSpec((1, past, WIDTH), lambda b: (b, 0, 0)),
            pl.BlockSpec((1, past, WIDTH), lambda b: (b, 0, 0)),
            pl.BlockSpec((1, N_HEADS, past), lambda b: (b, 0, 0)),
        ],
        out_specs=pl.BlockSpec((t, WIDTH), row),
        out_shape=jax.ShapeDtypeStruct((n, WIDTH), BF16),
        scratch_shapes=[pltpu.VMEM((past, past), BF16)],
        compiler_params=pltpu.CompilerParams(
            dimension_semantics=("arbitrary",), vmem_limit_bytes=VMEM_LIMIT_BYTES),
        name="fox_attention_step",
    )(qaug, kn, vn, logf, cache_k, cache_v, cache_logf_t)


def _out_kernel(x_ref, of_ref, orw_ref, wmix_ref, g_ref, wi_ref, wo_ref, o_ref):
    mix = lax.dot_general(of_ref[0], wmix_ref[:WIDTH, :], (((0,), (0,)), ((), ())), preferred_element_type=F32)
    mix = mix + jnp.dot(orw_ref[...], wmix_ref[WIDTH:, :], preferred_element_type=F32)
    o_ref[...] = _swiglu_half_step(x_ref[...] + mix, g_ref[...], wi_ref, wo_ref)


def _out_ffn(x, o_fox_t, o_rwkv, w_mix, g, wi, wo, tm):
    n = x.shape[0]
    per_fox_tile = o_fox_t.shape[2] // tm
    row = lambda i: (i, 0)
    return pl.pallas_call(
        _out_kernel,
        grid=(n // tm,),
        in_specs=[
            pl.BlockSpec((tm, D_MODEL), row),
            pl.BlockSpec((1, WIDTH, tm), lambda i: (i // per_fox_tile, 0, i % per_fox_tile)),
            pl.BlockSpec((tm, WIDTH), row),
            _const_spec((2 * WIDTH, D_MODEL)),
            _const_spec((1, D_MODEL)),
            _const_spec((D_MODEL, 2 * D_FF)),
            _const_spec((D_FF, D_MODEL)),
        ],
        out_specs=pl.BlockSpec((tm, D_MODEL), row),
        out_shape=jax.ShapeDtypeStruct((n, D_MODEL), F32),
        compiler_params=pltpu.CompilerParams(
            dimension_semantics=("arbitrary",), vmem_limit_bytes=VMEM_LIMIT_BYTES),
        name="out_proj_ffn",
    )(x, o_fox_t, o_rwkv, w_mix, g, wi, wo)


PROMPT_TILE = 512
ATTN_QUERY_TILE = 1024
RWKV_TILE = 256
RWKV_CHUNK = 64


def kernel(x_prompt, x_sample, cache_fox_k, cache_fox_v, cache_fox_logf, state_rwkv, state_rwkv_shift, norm_ffn1_g, ffn1_w_in, ffn1_w_out, norm_mix_g, w_in, w_out, fox_b_f, fox_q_norm_g, fox_k_norm_g, rwkv_mu, rwkv_w0, rwkv_w2, rwkv_a0, rwkv_a2, rwkv_g2, rwkv_k_k, rwkv_k_a, rwkv_r_k, rwkv_ln_g, rwkv_ln_b, norm_ffn2_g, ffn2_w_in, ffn2_w_out):
    depth = norm_ffn1_g.shape[0]
    bp, seq, _ = x_prompt.shape
    bs, dec_seq, _ = x_sample.shape
    assert bp == 1, "the prompt path assumes a single stream"
    n_p, n_s = bp * seq, bs * dec_seq
    yp = x_prompt.reshape(n_p, D_MODEL)
    ys = x_sample.reshape(n_s, D_MODEL)
    prompt_states = ([], [], [], [], [])
    sample_states = ([], [], [], [], [])
    for l in range(depth):
        ffn1 = (norm_ffn1_g[l][None], ffn1_w_in[l].astype(BF16), ffn1_w_out[l].astype(BF16))
        ffn2 = (norm_ffn2_g[l][None], ffn2_w_in[l].astype(BF16), ffn2_w_out[l].astype(BF16))
        proj_w = _prep_proj_weights(w_in[l], fox_b_f[l], fox_q_norm_g[l], fox_k_norm_g[l])
        rwkv_w = _prep_rwkv_weights(rwkv_mu[l], rwkv_w0[l], rwkv_w2[l], rwkv_a0[l], rwkv_a2[l], rwkv_g2[l],
                                    rwkv_k_k[l], rwkv_k_a[l], rwkv_r_k[l], rwkv_ln_g[l], rwkv_ln_b[l])
        w_mix = w_out[l].astype(BF16)
        mix_g = norm_mix_g[l][None]

        x1 = _ffn(yp, *ffn1, PROMPT_TILE)
        qaug, kn, kaug, v, vt, logf, prw, base = _proj(x1, mix_g, *proj_w, PROMPT_TILE)
        o_fox_t = _attention(base[:, 0, :N_HEADS].T, qaug, kaug, vt, ATTN_QUERY_TILE, PROMPT_TILE)
        o_rwkv, s_pairs = _rwkv(prw, jnp.zeros((bp, 1, RWKV_PROJ), F32),
                                jnp.zeros((bp, N_PAIRS, 128, 128), F32), rwkv_w, bp, RWKV_TILE, RWKV_CHUNK)
        yp = _out_ffn(x1, o_fox_t, o_rwkv, w_mix, *ffn2, PROMPT_TILE)
        for acc, st in zip(prompt_states, (
                kn.reshape(bp, seq, N_HEADS, HEAD_DIM), v.reshape(bp, seq, N_HEADS, HEAD_DIM),
                logf[:, :N_HEADS].reshape(bp, seq, N_HEADS), _pairs_to_state(s_pairs),
                prw.reshape(bp, seq, RWKV_PROJ)[:, -1:])):
            acc.append(st)

        x1 = _ffn(ys, *ffn1, n_s)
        qaug, kn, kaug, v, vt, logf, prw, base = _proj(x1, mix_g, *proj_w, n_s)
        past = cache_fox_k.shape[2]
        o_fox = _attention_step(
            qaug, kn, v, logf, cache_fox_k[l].reshape(bs, past, WIDTH), cache_fox_v[l].reshape(bs, past, WIDTH),
            jnp.swapaxes(cache_fox_logf[l], 1, 2), dec_seq)
        o_rwkv, s_pairs = _rwkv(prw, state_rwkv_shift[l], _state_to_pairs(state_rwkv[l]), rwkv_w,
                                bs, dec_seq, dec_seq)
        ys = _out_ffn(x1, o_fox.T[None], o_rwkv, w_mix, *ffn2, n_s)
        for acc, st in zip(sample_states, (
                kn.reshape(bs, dec_seq, N_HEADS, HEAD_DIM), v.reshape(bs, dec_seq, N_HEADS, HEAD_DIM),
                logf[:, :N_HEADS].reshape(bs, dec_seq, N_HEADS), _pairs_to_state(s_pairs),
                prw.reshape(bs, dec_seq, RWKV_PROJ)[:, -1:])):
            acc.append(st)

    return (yp.reshape(bp, seq, D_MODEL), ys.reshape(bs, dec_seq, D_MODEL),
            *(jnp.stack(a) for a in prompt_states), *(jnp.stack(a) for a in sample_states))
```

```python
import functools
import math

import jax
import jax.numpy as jnp
from jax import lax
from jax.experimental import pallas as pl
from jax.experimental.pallas import tpu as pltpu

F32 = jnp.float32
BF16 = jnp.bfloat16

D_MODEL = 1024
D_FF = 2816
HEAD_DIM = 64
N_HEADS = 8
WIDTH = N_HEADS * HEAD_DIM
RWKV_PROJ = 1792
RMS_EPS = 1e-6
GN_EPS = 64e-5
LOG2E = math.log2(math.e)

VMEM_LIMIT_BYTES = 56 * 1024 * 1024
FF_CHUNK = 256


def _const_spec(shape):
    return pl.BlockSpec(shape, lambda *_: (0,) * len(shape), pipeline_mode=pl.Buffered(1))


def _rms(x, g):
    return x * lax.rsqrt(jnp.mean(x * x, axis=-1, keepdims=True) + RMS_EPS) * g


def _swiglu_half_step(x, g, wi_ref, wo_ref):
    h = _rms(x, g).astype(BF16)
    acc = x
    for c0 in range(0, D_FF, FF_CHUNK):
        c1 = min(c0 + FF_CHUNK, D_FF)
        gate = jnp.dot(h, wi_ref[:, c0:c1], preferred_element_type=F32)
        up = jnp.dot(h, wi_ref[:, D_FF + c0:D_FF + c1], preferred_element_type=F32)
        a = (gate * jax.nn.sigmoid(gate) * up).astype(BF16)
        acc = acc + 0.5 * jnp.dot(a, wo_ref[c0:c1, :], preferred_element_type=F32)
    return acc


def _ffn_kernel(x_ref, g_ref, wi_ref, wo_ref, o_ref):
    o_ref[...] = _swiglu_half_step(x_ref[...], g_ref[...], wi_ref, wo_ref)


def _ffn(x, g, wi, wo, tm):
    n = x.shape[0]
    return pl.pallas_call(
        _ffn_kernel,
        grid=(n // tm,),
        in_specs=[
            pl.BlockSpec((tm, D_MODEL), lambda i: (i, 0)),
            _const_spec((1, D_MODEL)),
            _const_spec((D_MODEL, 2 * D_FF)),
            _const_spec((D_FF, D_MODEL)),
        ],
        out_specs=pl.BlockSpec((tm, D_MODEL), lambda i: (i, 0)),
        out_shape=jax.ShapeDtypeStruct((n, D_MODEL), F32),
        compiler_params=pltpu.CompilerParams(
            dimension_semantics=("arbitrary",), vmem_limit_bytes=VMEM_LIMIT_BYTES),
        name="ffn",
    )(x, g, wi, wo)


PROJ_COLS = 3 * WIDTH + RWKV_PROJ + 128
V_ROWS = HEAD_DIM + 16
N_SPLIT = 3
PAIR_LANES = 256
AUG_WIDTH = (N_HEADS // 2) * PAIR_LANES


def _split_bf16(x, n):
    parts = []
    for _ in range(n):
        p = x.astype(BF16)
        parts.append(p)
        x = x - p.astype(F32)
    return parts


def _head_sums(x, bd):
    return jnp.dot(x.astype(BF16), bd, preferred_element_type=F32)


def _log_sigmoid(x):
    return jnp.minimum(x, 0.0) - jnp.log1p(jnp.exp(-jnp.abs(x)))


def _proj_kernel(x_ref, g_ref, w_ref, bf_ref, qg_ref, kg_ref, bd_ref, eq_ref, ek_ref, oneq_ref, onek_ref,
                 qaug_ref, k_ref, kaug_ref, v_ref, vt_ref, logf_ref, prw_ref, base_ref, carry_ref):
    i = pl.program_id(0)
    tm = x_ref.shape[0]

    @pl.when(i == 0)
    def _():
        carry_ref[...] = jnp.zeros_like(carry_ref)

    h = _rms(x_ref[...], g_ref[...]).astype(BF16)
    proj = jnp.dot(h, w_ref[...], preferred_element_type=F32)
    q = proj[:, :WIDTH]
    k = proj[:, WIDTH:2 * WIDTH]
    v = proj[:, 2 * WIDTH:3 * WIDTH]
    prw_ref[...] = proj[:, 3 * WIDTH:3 * WIDTH + RWKV_PROJ]
    f = proj[:, 3 * WIDTH + RWKV_PROJ:]

    bd = bd_ref[...]
    qn = q * lax.rsqrt(_head_sums(q * q, bd) * (1.0 / HEAD_DIM) + RMS_EPS) * qg_ref[...]
    kn = k * lax.rsqrt(_head_sums(k * k, bd) * (1.0 / HEAD_DIM) + RMS_EPS) * kg_ref[...]
    k_ref[...] = kn
    v_ref[...] = v
    v_t = v.T.astype(BF16)
    for hd in range(N_HEADS):
        vt_ref[0, hd * V_ROWS:hd * V_ROWS + HEAD_DIM, :] = v_t[hd * HEAD_DIM:(hd + 1) * HEAD_DIM, :]
        vt_ref[0, hd * V_ROWS + HEAD_DIM:(hd + 1) * V_ROWS, :] = jnp.ones((V_ROWS - HEAD_DIM, tm), BF16)

    lane = lax.broadcasted_iota(jnp.int32, f.shape, 1)
    logf = jnp.where(lane < N_HEADS, _log_sigmoid(f + bf_ref[...]), 0.0)
    logf_ref[...] = logf

    row = lax.broadcasted_iota(jnp.int32, (tm, tm), 0)
    col = lax.broadcasted_iota(jnp.int32, (tm, tm), 1)
    tri = jnp.where(col <= row, 1.0, 0.0).astype(BF16)
    cl = jnp.zeros_like(logf)
    for part in _split_bf16(logf, N_SPLIT):
        cl = cl + jnp.dot(tri, part, preferred_element_type=F32)
    c2 = cl * LOG2E
    base_ref[0] = jnp.broadcast_to(carry_ref[...], base_ref.shape[1:])
    carry_ref[...] = carry_ref[...] + c2[tm - 1:tm, :]

    terms = _split_bf16(c2, N_SPLIT)
    parts = sum(pltpu.roll(t.astype(F32), N_HEADS * s, axis=1) if s else t.astype(F32)
                for s, t in enumerate(terms)).astype(BF16)
    aug_q = (jnp.dot(parts, eq_ref[...], preferred_element_type=F32) + oneq_ref[...]).astype(BF16)
    aug_k = (jnp.dot(parts, ek_ref[...], preferred_element_type=F32) + onek_ref[...]).astype(BF16)
    qs = (qn * (HEAD_DIM ** -0.5 * LOG2E)).astype(BF16)
    ks = kn.astype(BF16)
    for p in range(N_HEADS // 2):
        qaug_ref[:, p * PAIR_LANES:p * PAIR_LANES + 128] = qs[:, p * 128:(p + 1) * 128]
        qaug_ref[:, p * PAIR_LANES + 128:(p + 1) * PAIR_LANES] = aug_q[:, p * 128:(p + 1) * 128]
        kaug_ref[:, p * PAIR_LANES:p * PAIR_LANES + 128] = ks[:, p * 128:(p + 1) * 128]
        kaug_ref[:, p * PAIR_LANES + 128:(p + 1) * PAIR_LANES] = aug_k[:, p * 128:(p + 1) * 128]


def _carrier_constants():
    eq = [[0.0] * WIDTH for _ in range(128)]
    ek = [[0.0] * WIDTH for _ in range(128)]
    oneq = [0.0] * WIDTH
    onek = [0.0] * WIDTH
    for hd in range(N_HEADS):
        off = (hd // 2) * 128 + (hd % 2) * 8
        for s in range(N_SPLIT):
            eq[s * N_HEADS + hd][off + s] = 1.0
            ek[s * N_HEADS + hd][off + N_SPLIT + s] = -1.0
            oneq[off + N_SPLIT + s] = 1.0
            onek[off + s] = 1.0
    return (jnp.array(eq, BF16), jnp.array(ek, BF16), jnp.array([oneq], F32), jnp.array([onek], F32))


def _prep_proj_weights(w_in, fox_b_f, q_norm_g, k_norm_g):
    fox_cols = 3 * WIDTH
    w_f = jnp.pad(w_in[:, fox_cols:fox_cols + N_HEADS], ((0, 0), (0, 128 - N_HEADS)))
    w_all = jnp.concatenate([w_in[:, :fox_cols], w_in[:, fox_cols + N_HEADS:], w_f], axis=1).astype(BF16)
    bf_pad = jnp.pad(fox_b_f, (0, 128 - N_HEADS))[None]
    return w_all, bf_pad, jnp.tile(q_norm_g, N_HEADS)[None], jnp.tile(k_norm_g, N_HEADS)[None]


def _proj(x, g, w_all, bf_pad, qg, kg, tm):
    n = x.shape[0]
    nblk = n // tm
    seg = jnp.arange(WIDTH) // HEAD_DIM
    bd = (seg[:, None] == seg[None, :]).astype(BF16)
    eq, ek, oneq, onek = _carrier_constants()
    row = lambda i: (i, 0)
    out_shape = (
        jax.ShapeDtypeStruct((n, AUG_WIDTH), BF16),
        jax.ShapeDtypeStruct((n, WIDTH), F32),
        jax.ShapeDtypeStruct((n, AUG_WIDTH), BF16),
        jax.ShapeDtypeStruct((n, WIDTH), F32),
        jax.ShapeDtypeStruct((nblk, N_HEADS * V_ROWS, tm), BF16),
        jax.ShapeDtypeStruct((n, 128), F32),
        jax.ShapeDtypeStruct((n, RWKV_PROJ), F32),
        jax.ShapeDtypeStruct((nblk, 8, 128), F32),
    )
    out_specs = (
        pl.BlockSpec((tm, AUG_WIDTH), row),
        pl.BlockSpec((tm, WIDTH), row),
        pl.BlockSpec((tm, AUG_WIDTH), row),
        pl.BlockSpec((tm, WIDTH), row),
        pl.BlockSpec((1, N_HEADS * V_ROWS, tm), lambda i: (i, 0, 0)),
        pl.BlockSpec((tm, 128), row),
        pl.BlockSpec((tm, RWKV_PROJ), row),
        pl.BlockSpec((1, 8, 128), lambda i: (i, 0, 0)),
    )
    return pl.pallas_call(
        _proj_kernel,
        grid=(nblk,),
        in_specs=[
            pl.BlockSpec((tm, D_MODEL), row),
            _const_spec((1, D_MODEL)),
            _const_spec((D_MODEL, PROJ_COLS)),
            _const_spec((1, 128)),
            _const_spec((1, WIDTH)),
            _const_spec((1, WIDTH)),
            _const_spec((WIDTH, WIDTH)),
            _const_spec((128, WIDTH)),
            _const_spec((128, WIDTH)),
            _const_spec((1, WIDTH)),
            _const_spec((1, WIDTH)),
        ],
        out_specs=out_specs,
        out_shape=out_shape,
        scratch_shapes=[pltpu.VMEM((1, 128), F32)],
        compiler_params=pltpu.CompilerParams(
            dimension_semantics=("arbitrary",), vmem_limit_bytes=VMEM_LIMIT_BYTES),
        name="mix_proj",
    )(x, g, w_all, bf_pad, qg, kg, bd, eq, ek, oneq, onek)


NEG_BIG = -1e30
STRIP = 256


def _attn_kernel(base_ref, q_ref, k_ref, vt_ref, o_ref,
                 qt_ref, s0_ref, s1_ref, x0_ref, x1_ref, p0_ref, p1_ref, m_ref, alpha_ref, acc_ref, *, tk):
    pair = pl.program_id(0)
    qi = pl.program_id(1)
    tq = q_ref.shape[0]
    sub = tq // tk
    n_steps = (qi + 1) * sub
    assert sub in (1, 2), "the first (peeled) trip must hold every diagonal block"
    strips = [(e, h) for e in range(2) for h in range(tq // STRIP)]

    def cols(e, h):
        return slice(e * tq + h * STRIP, e * tq + (h + 1) * STRIP)

    def strip_id(e, h):
        return e * (tq // STRIP) + h

    lane = lax.broadcasted_iota(jnp.int32, (tq, PAIR_LANES), 1)
    q = q_ref[...].astype(F32)
    for e in range(2):
        f0 = e * HEAD_DIM
        c0 = 128 + e * 8
        keep = ((lane >= f0) & (lane < f0 + HEAD_DIM)) | ((lane >= c0) & (lane < c0 + 2 * N_SPLIT))
        qt_ref[:, e * tq:(e + 1) * tq] = jnp.where(keep, q, 0.0).T.astype(BF16)

    def block_of(step):
        return jnp.where(step < sub, qi * sub + step, step - sub)

    def offset(step, e, h):
        hd = 2 * pair + e
        d = base_ref[hd, qi * sub + (h * STRIP) // tk] - base_ref[hd, block_of(step)]
        return jnp.where(step < n_steps, d, NEG_BIG)

    r = lax.broadcasted_iota(jnp.int32, (tk, STRIP), 0)
    c = lax.broadcasted_iota(jnp.int32, (tk, STRIP), 1)

    def scores(step, s_ref, smax_ref, e, h, diag=None):
        blk = block_of(jnp.minimum(step, n_steps))
        kc = k_ref[pl.ds(pl.multiple_of(blk * tk, tk), tk), :]
        s = jnp.dot(kc, qt_ref[:, cols(e, h)], preferred_element_type=F32)
        if diag is not None:
            s = jnp.where(r + diag * tk <= c + h * STRIP, s, NEG_BIG)
        s_ref[strip_id(e, h)] = s
        smax_ref[:, cols(e, h)] = jnp.max(s, axis=0, keepdims=True)

    def masked_out(h, diag):
        return diag is not None and (h + 1) * STRIP <= diag * tk

    def softmax(step, s_ref, smax_ref, p_ref, diag=None):
        for e, h in strips:
            cs = cols(e, h)
            if masked_out(h, diag):
                alpha_ref[:, cs] = jnp.ones((1, STRIP), F32)
                p_ref[strip_id(e, h)] = jnp.zeros((tk, STRIP), BF16)
                continue
            d = offset(step, e, h)
            m_old = m_ref[:, cs]
            m_new = jnp.maximum(m_old, smax_ref[:, cs] + d)
            m_ref[:, cs] = m_new
            alpha_ref[:, cs] = jnp.exp2(m_old - m_new)
            p_ref[strip_id(e, h)] = jnp.exp2((s_ref[strip_id(e, h)] - (m_new - d)).astype(BF16))

    def accumulate(step, p_ref, e, h):
        vt = vt_ref[block_of(step), e * V_ROWS:(e + 1) * V_ROWS, :]
        pv = jnp.dot(vt, p_ref[strip_id(e, h)], preferred_element_type=F32)
        hs = slice(h * STRIP, (h + 1) * STRIP)
        acc_ref[e, :, hs] = alpha_ref[:, cols(e, h)] * acc_ref[e, :, hs] + pv

    def two_steps(k, first=False):
        diag = 1 if (first and sub > 1) else None
        for e, h in strips:
            if not first:
                accumulate(k - 1, p1_ref, e, h)
            if not masked_out(h, diag):
                scores(k + 1, s1_ref, x1_ref, e, h, diag=diag)
        softmax(k, s0_ref, x0_ref, p0_ref)
        for e, h in strips:
            accumulate(k, p0_ref, e, h)
            scores(k + 2, s0_ref, x0_ref, e, h)
        softmax(k + 1, s1_ref, x1_ref, p1_ref, diag=diag)

    m_ref[...] = jnp.full(m_ref.shape, NEG_BIG, F32)
    acc_ref[...] = jnp.zeros_like(acc_ref)
    for e, h in strips:
        scores(0, s0_ref, x0_ref, e, h, diag=0)
    two_steps(0, first=True)

    def trip(it, carry):
        two_steps(2 * it)
        return carry

    n_trips = (n_steps + 1) // 2
    lax.fori_loop(1, n_trips, trip, 0)
    for e, h in strips:
        accumulate(2 * n_trips - 1, p1_ref, e, h)
    for e in range(2):
        acc = acc_ref[e]
        o_ref[0, e * HEAD_DIM:(e + 1) * HEAD_DIM, :] = (
            acc[:HEAD_DIM] / acc[HEAD_DIM:HEAD_DIM + 1]).astype(o_ref.dtype)


def _attention(base, qaug, kaug, vt, tq, tk):
    n = qaug.shape[0]
    grid_spec = pltpu.PrefetchScalarGridSpec(
        num_scalar_prefetch=1,
        grid=(N_HEADS // 2, n // tq),
        in_specs=[
            pl.BlockSpec((tq, PAIR_LANES), lambda p, i, b: (i, p)),
            pl.BlockSpec((n, PAIR_LANES), lambda p, i, b: (0, p)),
            pl.BlockSpec((n // tk, 2 * V_ROWS, tk), lambda p, i, b: (0, p, 0)),
        ],
        out_specs=pl.BlockSpec((1, 2 * HEAD_DIM, tq), lambda p, i, b: (i, p, 0)),
        scratch_shapes=[
            pltpu.VMEM((PAIR_LANES, 2 * tq), BF16),
            pltpu.VMEM((2 * tq // STRIP, tk, STRIP), F32),
            pltpu.VMEM((2 * tq // STRIP, tk, STRIP), F32),
            pltpu.VMEM((1, 2 * tq), F32),
            pltpu.VMEM((1, 2 * tq), F32),
            pltpu.VMEM((2 * tq // STRIP, tk, STRIP), BF16),
            pltpu.VMEM((2 * tq // STRIP, tk, STRIP), BF16),
            pltpu.VMEM((1, 2 * tq), F32),
            pltpu.VMEM((1, 2 * tq), F32),
            pltpu.VMEM((2, V_ROWS, tq), F32),
        ],
    )
    return pl.pallas_call(
        functools.partial(_attn_kernel, tk=tk),
        grid_spec=grid_spec,
        out_shape=jax.ShapeDtypeStruct((n // tq, WIDTH, tq), BF16),
        compiler_params=pltpu.CompilerParams(
            dimension_semantics=("arbitrary", "arbitrary"), vmem_limit_bytes=VMEM_LIMIT_BYTES),
        name="fox_attention",
    )(base, qaug, kaug, vt)


N_PAIRS = N_HEADS // 2


def _mm(a, b):
    return jnp.dot(a.astype(BF16), b.astype(BF16), preferred_element_type=F32)


def _mm_nt(a, b):
    return lax.dot_general(a.astype(BF16), b.astype(BF16), (((1,), (1,)), ((), ())), preferred_element_type=F32)


def _mm_tn(a, b):
    return lax.dot_general(a.astype(BF16), b.astype(BF16), (((0,), (0,)), ((), ())), preferred_element_type=F32)


def _rwkv_kernel(prw_ref, shift0_ref, s0_ref, mu_ref, w0_ref, a0_ref, w2a2_ref, g2_ref, kk_ref, ka_ref, rk_ref,
                 lng_ref, lnb_ref, bd_ref,
                 o_ref, sout_ref,
                 prev_ref, h_ref, y_ref,
                 *, chunk):
    t = pl.program_id(1)
    nt = pl.num_programs(1)
    tr = prw_ref.shape[0]
    n_chunks = tr // chunk

    @pl.when(t == 0)
    def _():
        prev_ref[...] = shift0_ref[0]
        h_ref[...] = s0_ref[0]

    prw = prw_ref[...]
    rolled = pltpu.roll(prw, 1, axis=0)
    row = lax.broadcasted_iota(jnp.int32, prw.shape, 0)
    prev = jnp.where(row == 0, prev_ref[...], rolled)
    prev_ref[...] = prw[tr - 1:tr, :]
    xs = prw + (prev - prw) * mu_ref[...]

    r = xs[:, :WIDTH]
    kr = xs[:, WIDTH:2 * WIDTH]
    vr = xs[:, 2 * WIDTH:3 * WIDTH]
    wa = xs[:, 3 * WIDTH:3 * WIDTH + 128]
    gd = xs[:, 3 * WIDTH + 128:]
    lane128 = lax.broadcasted_iota(jnp.int32, wa.shape, 1)
    wa_in = jnp.where(lane128 < 64, jnp.tanh(wa), wa).astype(BF16)
    lora = jnp.dot(wa_in, w2a2_ref[...], preferred_element_type=F32)
    w_log = _log_sigmoid(w0_ref[...] + lora[:, :WIDTH]) - 0.5
    lw = -jnp.exp(w_log)
    lr = jax.nn.sigmoid(a0_ref[...] + lora[:, WIDTH:])
    gate = jnp.dot(jax.nn.sigmoid(gd).astype(BF16), g2_ref[...], preferred_element_type=F32)

    bd = bd_ref[...]
    kk = kr * kk_ref[...]
    kk = kk / jnp.maximum(jnp.sqrt(_head_sums(kk * kk, bd)), 1e-12)
    k = kr * (1.0 + (lr - 1.0) * ka_ref[...])
    b = kk * lr

    ri = lax.broadcasted_iota(jnp.int32, (tr, tr), 0)
    ci = lax.broadcasted_iota(jnp.int32, (tr, tr), 1)
    tri = jnp.where((ci <= ri) & (ci // chunk == ri // chunk), 1.0, 0.0).astype(BF16)
    cs = jnp.zeros_like(lw)
    for part in _split_bf16(lw, N_SPLIT):
        cs = cs + jnp.dot(tri, part, preferred_element_type=F32)
    e_pos = jnp.exp(cs)
    e_neg = jnp.exp(-cs)
    at = (-kk * jnp.exp(cs - lw)).astype(BF16)
    rt = (r * e_pos).astype(BF16)
    bt = (b * e_neg).astype(BF16)
    kt = (k * e_neg).astype(BF16)
    vb = vr.astype(BF16)
    g_last, bh_rows, kh_rows = [], [], []
    for c in range(n_chunks):
        rows = slice(c * chunk, (c + 1) * chunk)
        last = cs[(c + 1) * chunk - 1:(c + 1) * chunk, :]
        to_end = jnp.exp(last - cs[rows, :])
        bh_rows.append((b[rows, :] * to_end).astype(BF16))
        kh_rows.append((k[rows, :] * to_end).astype(BF16))
        g_last.append(jnp.exp(last))

    c2 = 2 * chunk
    lane = lax.broadcasted_iota(jnp.int32, (chunk, 128), 1)
    even = lane < HEAD_DIM

    def stack(x, c, p):
        x = x[c * chunk:(c + 1) * chunk, p * 128:(p + 1) * 128]
        zero = jnp.zeros_like(x)
        return jnp.concatenate([jnp.where(even, x, zero), jnp.where(even, zero, x)], axis=0)

    rr = lax.broadcasted_iota(jnp.int32, (c2, c2), 0)
    cc = lax.broadcasted_iota(jnp.int32, (c2, c2), 1)
    same = (rr // chunk) == (cc // chunk)
    strict = same & (cc < rr)
    incl = same & (cc <= rr)
    eye = jnp.where(rr == cc, 1.0, 0.0)
    r128 = lax.broadcasted_iota(jnp.int32, (128, 128), 0)
    c128 = lax.broadcasted_iota(jnp.int32, (128, 128), 1)
    diag128 = r128 == c128

    insts = [(c, p) for c in range(n_chunks) for p in range(N_PAIRS)]
    a_s = [stack(at, c, p) for c, p in insts]
    r_s = [stack(rt, c, p) for c, p in insts]
    b_s = [stack(bt, c, p) for c, p in insts]
    k_s = [stack(kt, c, p) for c, p in insts]
    v_s = [stack(vb, c, p) for c, p in insts]
    bh_s = [stack(bh_rows[c], 0, p) for c, p in insts]
    kh_s = [stack(kh_rows[c], 0, p) for c, p in insts]
    wide = c2 % 128 == 0

    def pair(mm, lhs, r1, r2, axis=1):
        if not wide:
            return mm(lhs, r1), mm(lhs, r2)
        out = mm(lhs, jnp.concatenate([r1.astype(BF16), r2.astype(BF16)], axis=axis))
        return out[:, :out.shape[1] // 2], out[:, out.shape[1] // 2:]

    ar_s = [jnp.concatenate([a, rr_], axis=0) for a, rr_ in zip(a_s, r_s)]
    gbk = [pair(_mm_nt, x, y, z, axis=0) for x, y, z in zip(ar_s, b_s, k_s)]
    a_ab = [jnp.where(strict, gb[:c2], 0.0) for gb, _ in gbk]
    a_ak = [jnp.where(strict, gk[:c2], 0.0).astype(BF16) for _, gk in gbk]
    a_rb = [jnp.where(incl, gb[c2:], 0.0).astype(BF16) for gb, _ in gbk]
    a_rk = [jnp.where(incl, gk[c2:], 0.0).astype(BF16) for _, gk in gbk]
    x1 = [_mm(m, v) for m, v in zip(a_ak, v_s)]
    levels = int(math.log2(chunk))
    pw = [x.astype(BF16) for x in a_ab]
    inv = [eye + x for x in a_ab]
    pw = [_mm(x, x).astype(BF16) for x in pw]
    for j in range(1, levels):
        if j < levels - 1:
            sq_inc = [pair(_mm, x, x, i) for x, i in zip(pw, inv)]
            pw = [sq.astype(BF16) for sq, _ in sq_inc]
            inv = [i + inc for i, (_, inc) in zip(inv, sq_inc)]
        else:
            inv = [i + _mm(x, i) for x, i in zip(pw, inv)]
    inv = [i.astype(BF16) for i in inv]
    wm_u0 = [pair(_mm, i, a, x) for i, a, x in zip(inv, a_s, x1)]
    wm = [w.astype(BF16) for w, _ in wm_u0]
    u0 = [u.astype(BF16) for _, u in wm_u0]
    bh_wu = [pair(_mm_tn, bh, w, u) for bh, w, u in zip(bh_s, wm, u0)]
    rb_wu = [pair(_mm, m, w, u) for m, w, u in zip(a_rb, wm, u0)]
    mb = [x.astype(BF16) for x, _ in bh_wu]
    rm = [(rr_.astype(F32) + x).astype(BF16) for rr_, (x, _) in zip(r_s, rb_wu)]
    y0 = [x + _mm(m2, v) for (_, x), m2, v in zip(rb_wu, a_rk, v_s)]
    n0 = [x + _mm_tn(kh, v) for (_, x), kh, v in zip(bh_wu, kh_s, v_s)]

    states = [h_ref[p] for p in range(N_PAIRS)]
    for i, (c, p) in enumerate(insts):
        h = states[p]
        hb = h.astype(BF16)
        y = _mm(rm[i], hb) + y0[i]
        y_ref[c * chunk:(c + 1) * chunk, p * 128:(p + 1) * 128] = y[:chunk] + y[chunk:]
        g_col = jnp.sum(jnp.where(diag128, g_last[c][:, p * 128:(p + 1) * 128], 0.0), axis=1, keepdims=True)
        states[p] = g_col * h + _mm(mb[i], hb) + n0[i]
    for p in range(N_PAIRS):
        h_ref[p] = states[p]

    y = y_ref[...]
    mean = _head_sums(y, bd) * (1.0 / HEAD_DIM)
    yc = y - mean
    var = _head_sums(yc * yc, bd) * (1.0 / HEAD_DIM)
    yn = yc * lax.rsqrt(var + GN_EPS) * lng_ref[...] + lnb_ref[...]
    bonus = _head_sums(r * k * rk_ref[...], bd) * vr
    o_ref[...] = ((yn + bonus) * gate).astype(o_ref.dtype)

    @pl.when(t == nt - 1)
    def _():
        sout_ref[0] = h_ref[...]


def _rwkv(prw, shift0, s0bd, wts, nseq, tr, chunk):
    n = prw.shape[0]
    tiles = n // (nseq * tr)
    row = lambda s, t: (s * tiles + t, 0)
    vec = lambda w: _const_spec((1, w))
    scratch = [pltpu.VMEM((1, RWKV_PROJ), F32), pltpu.VMEM((N_PAIRS, 128, 128), F32),
               pltpu.VMEM((tr, WIDTH), F32)]
    return pl.pallas_call(
        functools.partial(_rwkv_kernel, chunk=chunk),
        grid=(nseq, tiles),
        in_specs=[
            pl.BlockSpec((tr, RWKV_PROJ), row),
            pl.BlockSpec((1, 1, RWKV_PROJ), lambda s, t: (s, 0, 0)),
            pl.BlockSpec((1, N_PAIRS, 128, 128), lambda s, t: (s, 0, 0, 0)),
            vec(RWKV_PROJ), vec(WIDTH), vec(WIDTH),
            _const_spec((128, 2 * WIDTH)), _const_spec((128, WIDTH)),
            vec(WIDTH), vec(WIDTH), vec(WIDTH), vec(WIDTH), vec(WIDTH),
            _const_spec((WIDTH, WIDTH)),
        ],
        out_specs=(
            pl.BlockSpec((tr, WIDTH), row),
            pl.BlockSpec((1, N_PAIRS, 128, 128), lambda s, t: (s, 0, 0, 0)),
        ),
        out_shape=(
            jax.ShapeDtypeStruct((n, WIDTH), BF16),
            jax.ShapeDtypeStruct((nseq, N_PAIRS, 128, 128), F32),
        ),
        scratch_shapes=scratch,
        compiler_params=pltpu.CompilerParams(
            dimension_semantics=("arbitrary", "arbitrary"), vmem_limit_bytes=VMEM_LIMIT_BYTES),
        name="rwkv7_mix",
    )(prw, shift0, s0bd, *wts)


def _prep_rwkv_weights(mu, w0, w2, a0, a2, g2, k_k, k_a, r_k, ln_g, ln_b):
    z = jnp.zeros((64, WIDTH), F32)
    w2a2 = jnp.concatenate([jnp.concatenate([w2, z], axis=1), jnp.concatenate([z, a2], axis=1)], axis=0)
    seg = jnp.arange(WIDTH) // HEAD_DIM
    bd = (seg[:, None] == seg[None, :]).astype(BF16)
    return (mu[None], w0[None], a0[None], w2a2.astype(BF16), g2.astype(BF16), k_k[None], k_a[None],
            r_k.reshape(1, WIDTH), ln_g[None], ln_b[None], bd)


def _state_to_pairs(s):
    n = s.shape[0]
    ht = jnp.swapaxes(s, -1, -2).reshape(n, N_PAIRS, 2, HEAD_DIM, HEAD_DIM)
    z = jnp.zeros_like(ht[:, :, 0])
    top = jnp.concatenate([ht[:, :, 0], z], axis=-1)
    bot = jnp.concatenate([z, ht[:, :, 1]], axis=-1)
    return jnp.concatenate([top, bot], axis=-2)


def _pairs_to_state(hbd):
    n = hbd.shape[0]
    even = hbd[:, :, :HEAD_DIM, :HEAD_DIM]
    odd = hbd[:, :, HEAD_DIM:, HEAD_DIM:]
    ht = jnp.stack([even, odd], axis=2).reshape(n, N_HEADS, HEAD_DIM, HEAD_DIM)
    return jnp.swapaxes(ht, -1, -2)


def _attn_step_kernel(q_ref, kn_ref, vn_ref, lf_ref, ck_ref, cv_ref, clf_ref, o_ref, after_ref):
    t = q_ref.shape[0]
    past = ck_ref.shape[1]

    @pl.when(pl.program_id(0) == 0)
    def _():
        mi = lax.broadcasted_iota(jnp.int32, (past, past), 0)
        ji = lax.broadcasted_iota(jnp.int32, (past, past), 1)
        after_ref[...] = jnp.where(mi > ji, 1.0, 0.0).astype(BF16)

    after = after_ref[...]
    suffix = jnp.zeros(clf_ref.shape[1:], F32)
    for part in _split_bf16(clf_ref[0], N_SPLIT):
        suffix = suffix + jnp.dot(part, after, preferred_element_type=F32)
    suffix = suffix * LOG2E
    ri = lax.broadcasted_iota(jnp.int32, (t, t), 0)
    ci = lax.broadcasted_iota(jnp.int32, (t, t), 1)
    causal = ci <= ri
    tri = jnp.where(causal, 1.0, 0.0).astype(BF16)
    cn = jnp.zeros(lf_ref.shape, F32)
    for part in _split_bf16(lf_ref[...], N_SPLIT):
        cn = cn + jnp.dot(tri, part, preferred_element_type=F32)
    cn = cn * LOG2E
    cn_t = cn.T

    lane = lax.broadcasted_iota(jnp.int32, (t, 128), 1)
    for p in range(N_PAIRS):
        q_pair = q_ref[:, p * PAIR_LANES:p * PAIR_LANES + 128]
        lanes = slice(p * 128, (p + 1) * 128)
        k_past = ck_ref[0, :, lanes].astype(BF16)
        v_past = cv_ref[0, :, lanes].astype(BF16)
        k_new = kn_ref[:, lanes].astype(BF16)
        v_new = vn_ref[:, lanes].astype(BF16)
        outs = []
        for e in range(2):
            hd = 2 * p + e
            mine = (lane >= e * HEAD_DIM) & (lane < (e + 1) * HEAD_DIM)
            qm = jnp.where(mine, q_pair, jnp.zeros_like(q_pair))
            cq = cn[:, hd:hd + 1]
            s_past = lax.dot_general(qm, k_past, (((1,), (1,)), ((), ())), preferred_element_type=F32)
            s_past = s_past + cq + suffix[hd:hd + 1, :]
            s_new = lax.dot_general(qm, k_new, (((1,), (1,)), ((), ())), preferred_element_type=F32)
            s_new = jnp.where(causal, s_new + cq - cn_t[hd:hd + 1, :], NEG_BIG)
            m = jnp.maximum(jnp.max(s_past, axis=1, keepdims=True), jnp.max(s_new, axis=1, keepdims=True))
            p_past = jnp.exp2(s_past - m)
            p_new = jnp.exp2(s_new - m)
            denom = jnp.sum(p_past, axis=1, keepdims=True) + jnp.sum(p_new, axis=1, keepdims=True)
            pv = (jnp.dot(p_past.astype(BF16), v_past, preferred_element_type=F32)
                  + jnp.dot(p_new.astype(BF16), v_new, preferred_element_type=F32))
            outs.append(pv / denom)
        o_ref[:, lanes] = jnp.where(lane < HEAD_DIM, outs[0], outs[1]).astype(o_ref.dtype)


def _attention_step(qaug, kn, vn, logf, cache_k, cache_v, cache_logf_t, t):
    n = qaug.shape[0]
    nseq = n // t
    past = cache_k.shape[1]
    row = lambda b: (b, 0)
    return pl.pallas_call(
        _attn_step_kernel,
        grid=(nseq,),
        in_specs=[
            pl.BlockSpec((t, AUG_WIDTH), row),
            pl.BlockSpec((t, WIDTH), row),
            pl.BlockSpec((t, WIDTH), row),
            pl.BlockSpec((t, 128), row),
            pl.BlockSpec((1, past, WIDTH), lambda b: (b, 0, 0)),
            pl.BlockSpec((1, past, WIDTH), lambda b: (b, 0, 0)),
            pl.BlockSpec((1, N_HEADS, past), lambda b: (b, 0, 0)),
        ],
        out_specs=pl.BlockSpec((t, WIDTH), row),
        out_shape=jax.ShapeDtypeStruct((n, WIDTH), BF16),
        scratch_shapes=[pltpu.VMEM((past, past), BF16)],
        compiler_params=pltpu.CompilerParams(
            dimension_semantics=("arbitrary",), vmem_limit_bytes=VMEM_LIMIT_BYTES),
        name="fox_attention_step",
    )(qaug, kn, vn, logf, cache_k, cache_v, cache_logf_t)


def _out_kernel(x_ref, of_ref, orw_ref, wmix_ref, g_ref, wi_ref, wo_ref, o_ref):
    mix = lax.dot_general(of_ref[0], wmix_ref[:WIDTH, :], (((0,), (0,)), ((), ())), preferred_element_type=F32)
    mix = mix + jnp.dot(orw_ref[...], wmix_ref[WIDTH:, :], preferred_element_type=F32)
    o_ref[...] = _swiglu_half_step(x_ref[...] + mix, g_ref[...], wi_ref, wo_ref)


def _out_ffn(x, o_fox_t, o_rwkv, w_mix, g, wi, wo, tm):
    n = x.shape[0]
    per_fox_tile = o_fox_t.shape[2] // tm
    row = lambda i: (i, 0)
    return pl.pallas_call(
        _out_kernel,
        grid=(n // tm,),
        in_specs=[
            pl.BlockSpec((tm, D_MODEL), row),
            pl.BlockSpec((1, WIDTH, tm), lambda i: (i // per_fox_tile, 0, i % per_fox_tile)),
            pl.BlockSpec((tm, WIDTH), row),
            _const_spec((2 * WIDTH, D_MODEL)),
            _const_spec((1, D_MODEL)),
            _const_spec((D_MODEL, 2 * D_FF)),
            _const_spec((D_FF, D_MODEL)),
        ],
        out_specs=pl.BlockSpec((tm, D_MODEL), row),
        out_shape=jax.ShapeDtypeStruct((n, D_MODEL), F32),
        compiler_params=pltpu.CompilerParams(
            dimension_semantics=("arbitrary",), vmem_limit_bytes=VMEM_LIMIT_BYTES),
        name="out_proj_ffn",
    )(x, o_fox_t, o_rwkv, w_mix, g, wi, wo)


PROMPT_TILE = 512
ATTN_QUERY_TILE = 1024
RWKV_TILE = 256
RWKV_CHUNK = 64


def kernel(x_prompt, x_sample, cache_fox_k, cache_fox_v, cache_fox_logf, state_rwkv, state_rwkv_shift, norm_ffn1_g, ffn1_w_in, ffn1_w_out, norm_mix_g, w_in, w_out, fox_b_f, fox_q_norm_g, fox_k_norm_g, rwkv_mu, rwkv_w0, rwkv_w2, rwkv_a0, rwkv_a2, rwkv_g2, rwkv_k_k, rwkv_k_a, rwkv_r_k, rwkv_ln_g, rwkv_ln_b, norm_ffn2_g, ffn2_w_in, ffn2_w_out):
    depth = norm_ffn1_g.shape[0]
    bp, seq, _ = x_prompt.shape
    bs, dec_seq, _ = x_sample.shape
    assert bp == 1, "the prompt path assumes a single stream"
    n_p, n_s = bp * seq, bs * dec_seq
    yp = x_prompt.reshape(n_p, D_MODEL)
    ys = x_sample.reshape(n_s, D_MODEL)
    prompt_states = ([], [], [], [], [])
    sample_states = ([], [], [], [], [])
    for l in range(depth):
        ffn1 = (norm_ffn1_g[l][None], ffn1_w_in[l].astype(BF16), ffn1_w_out[l].astype(BF16))
        ffn2 = (norm_ffn2_g[l][None], ffn2_w_in[l].astype(BF16), ffn2_w_out[l].astype(BF16))
        proj_w = _prep_proj_weights(w_in[l], fox_b_f[l], fox_q_norm_g[l], fox_k_norm_g[l])
        rwkv_w = _prep_rwkv_weights(rwkv_mu[l], rwkv_w0[l], rwkv_w2[l], rwkv_a0[l], rwkv_a2[l], rwkv_g2[l],
                                    rwkv_k_k[l], rwkv_k_a[l], rwkv_r_k[l], rwkv_ln_g[l], rwkv_ln_b[l])
        w_mix = w_out[l].astype(BF16)
        mix_g = norm_mix_g[l][None]

        x1 = _ffn(yp, *ffn1, PROMPT_TILE)
        qaug, kn, kaug, v, vt, logf, prw, base = _proj(x1, mix_g, *proj_w, PROMPT_TILE)
        o_fox_t = _attention(base[:, 0, :N_HEADS].T, qaug, kaug, vt, ATTN_QUERY_TILE, PROMPT_TILE)
        o_rwkv, s_pairs = _rwkv(prw, jnp.zeros((bp, 1, RWKV_PROJ), F32),
                                jnp.zeros((bp, N_PAIRS, 128, 128), F32), rwkv_w, bp, RWKV_TILE, RWKV_CHUNK)
        yp = _out_ffn(x1, o_fox_t, o_rwkv, w_mix, *ffn2, PROMPT_TILE)
        for acc, st in zip(prompt_states, (
                kn.reshape(bp, seq, N_HEADS, HEAD_DIM), v.reshape(bp, seq, N_HEADS, HEAD_DIM),
                logf[:, :N_HEADS].reshape(bp, seq, N_HEADS), _pairs_to_state(s_pairs),
                prw.reshape(bp, seq, RWKV_PROJ)[:, -1:])):
            acc.append(st)

        x1 = _ffn(ys, *ffn1, n_s)
        qaug, kn, kaug, v, vt, logf, prw, base = _proj(x1, mix_g, *proj_w, n_s)
        past = cache_fox_k.shape[2]
        o_fox = _attention_step(
            qaug, kn, v, logf, cache_fox_k[l].reshape(bs, past, WIDTH), cache_fox_v[l].reshape(bs, past, WIDTH),
            jnp.swapaxes(cache_fox_logf[l], 1, 2), dec_seq)
        o_rwkv, s_pairs = _rwkv(prw, state_rwkv_shift[l], _state_to_pairs(state_rwkv[l]), rwkv_w,
                                bs, dec_seq, dec_seq)
        ys = _out_ffn(x1, o_fox.T[None], o_rwkv, w_mix, *ffn2, n_s)
        for acc, st in zip(sample_states, (
                kn.reshape(bs, dec_seq, N_HEADS, HEAD_DIM), v.reshape(bs, dec_seq, N_HEADS, HEAD_DIM),
                logf[:, :N_HEADS].reshape(bs, dec_seq, N_HEADS), _pairs_to_state(s_pairs),
                prw.reshape(bs, dec_seq, RWKV_PROJ)[:, -1:])):
            acc.append(st)

    return (yp.reshape(bp, seq, D_MODEL), ys.reshape(bs, dec_seq, D_MODEL),
            *(jnp.stack(a) for a in prompt_states), *(jnp.stack(a) for a in sample_states))
```
